```python
import jax, jax.numpy as jnp
from jax import lax
import numpy as np

D_MODEL = 1024
BATCH = 4
SEQ = 8192
DEPTH = 1

D_MIX = D_MODEL
D_POOL = D_MIX // 2
D_MLSTM = D_MIX - D_POOL
POOL_WINDOWS = (2, 4, 8, 16)
N_POOL_GROUPS = len(POOL_WINDOWS)
POOL_GROUP_DIM = D_POOL // N_POOL_GROUPS
N_HEADS = 4
HEAD_DIM = D_MLSTM // N_HEADS
CONV_WIDTH = 4
CHUNK = 128
EPS = 1e-6
OFF_U = 0
OFF_ZP = OFF_U + D_POOL
OFF_Q = OFF_ZP + D_POOL
OFF_K = OFF_Q + D_MLSTM
OFF_V = OFF_K + D_MLSTM
OFF_O = OFF_V + D_MLSTM
OFF_ZM = OFF_O + D_MLSTM
OFF_G = OFF_ZM + D_MLSTM
D_IN_PROJ = OFF_G + 2 * N_HEADS

kernel_name = "hymba_pool_mlstm_adaln_block"


def rms_norm(x, g):
    xf = x.astype(jnp.float32)
    y = xf * lax.rsqrt(jnp.mean(xf * xf, axis=-1, keepdims=True) + EPS)
    return (y * g.astype(jnp.float32)).astype(x.dtype)


def causal_pool_mixer(u, w_pool, ls_pool):
    B, S, _ = u.shape
    ug = u.astype(jnp.float32).reshape(B, S, N_POOL_GROUPS, POOL_GROUP_DIM)
    cs = jnp.cumsum(ug, axis=1)
    pos = jnp.arange(1, S + 1, dtype=jnp.float32)
    outs = []
    for g, w in enumerate(POOL_WINDOWS):
        c_g = cs[:, :, g]
        lag = jnp.pad(c_g[:, :S - w], ((0, 0), (w, 0), (0, 0)))
        mean = (c_g - lag) / jnp.minimum(pos, float(w))[None, :, None]
        outs.append(mean - ug[:, :, g])
    pooled = jnp.stack(outs, axis=2)
    y = jnp.einsum('bsgc,gcd->bsgd', pooled, w_pool.astype(jnp.float32))
    y = y.reshape(B, S, D_POOL) * ls_pool.astype(jnp.float32)
    return y.astype(u.dtype)


def causal_depthwise_conv(x, w, b):
    S = x.shape[1]
    xp = jnp.pad(x, ((0, 0), (CONV_WIDTH - 1, 0), (0, 0)))
    y = b
    for j in range(CONV_WIDTH):
        y = y + w[j] * xp[:, j:j + S]
    return y


def _mlstm_chunk_step(carry, inp):
    C, n, m = carry
    q, k, v, ig, lf = inp
    L = q.shape[2]
    b = jnp.cumsum(lf, axis=-1)
    causal = jnp.tril(jnp.ones((L, L), dtype=bool))
    d_log = jnp.where(causal, b[..., :, None] - b[..., None, :] + ig[..., None, :], -jnp.inf)
    inter_log = b + m[..., None]
    m_t = jnp.maximum(inter_log, jnp.max(d_log, axis=-1))
    dmat = jnp.exp(d_log - m_t[..., None])
    inter = jnp.exp(inter_log - m_t)
    s = jnp.einsum('bhtd,bhsd->bhts', q, k) * dmat
    num = jnp.einsum('bhts,bhsd->bhtd', s, v) + inter[..., None] * jnp.einsum('bhvd,bhtd->bhtv', C, q)
    den = jnp.sum(s, axis=-1) + inter * jnp.einsum('bhd,bhtd->bht', n, q)
    h = num / jnp.maximum(jnp.abs(den), jnp.exp(-m_t))[..., None]
    b_last = b[..., -1]
    w_log = b_last[..., None] - b + ig
    m_new = jnp.maximum(b_last + m, jnp.max(w_log, axis=-1))
    w = jnp.exp(w_log - m_new[..., None])
    decay = jnp.exp(b_last + m - m_new)
    C_new = decay[..., None, None] * C + jnp.einsum('bhs,bhsv,bhsd->bhvd', w, v, k)
    n_new = decay[..., None] * n + jnp.einsum('bhs,bhsd->bhd', w, k)
    return (C_new, n_new, m_new), h


def _to_chunks(a):
    B, S = a.shape[:2]
    a = a.reshape((B, S // CHUNK, CHUNK) + a.shape[2:])
    return jnp.swapaxes(jnp.moveaxis(a, 1, 0), 2, 3)


def mlstm_chunkwise(q, k, v, ig, lf):
    B, S, H, dh = q.shape
    f32 = jnp.float32
    k = k / jnp.sqrt(jnp.asarray(dh, f32)).astype(k.dtype)
    xs = tuple(_to_chunks(a.astype(f32)) for a in (q, k, v, ig, lf))
    carry0 = (jnp.zeros((B, H, dh, dh), f32), jnp.zeros((B, H, dh), f32), jnp.zeros((B, H), f32))
    _, hs = lax.scan(_mlstm_chunk_step, carry0, xs)
    hs = jnp.moveaxis(jnp.swapaxes(hs, 2, 3), 0, 1)
    return hs.reshape(B, S, H, dh)


def setup_inputs(seed: int = 0) -> dict:
    key = jax.random.key(seed)
    ks = jax.random.split(key, 16)
    nrm = jax.random.normal
    f32 = jnp.float32
    x = nrm(ks[0], (BATCH, SEQ, D_MODEL), f32)
    c = nrm(ks[1], (BATCH, D_MODEL), f32)
    norm_g = 1.0 + 0.02 * nrm(ks[2], (DEPTH, D_MODEL), f32)
    w_ada = 0.5 * D_MODEL ** -0.5 * nrm(ks[3], (DEPTH, D_MODEL, 3 * D_MODEL), f32)
    b_ada = 0.01 * nrm(ks[4], (DEPTH, 3 * D_MODEL), f32)
    w_in = D_MODEL ** -0.5 * nrm(ks[5], (DEPTH, D_MODEL, D_IN_PROJ), f32)
    b_igate = 0.1 * nrm(ks[6], (DEPTH, N_HEADS), f32)
    b_fgate = jnp.linspace(3.0, 6.0, N_HEADS, dtype=f32)[None, :] + 0.1 * nrm(ks[7], (DEPTH, N_HEADS), f32)
    b_gates = jnp.concatenate([b_igate, b_fgate], axis=-1)
    conv_w = 0.5 * nrm(ks[8], (DEPTH, CONV_WIDTH, 2 * D_MLSTM), f32)
    conv_b = 0.02 * nrm(ks[9], (DEPTH, 2 * D_MLSTM), f32)
    w_pool = POOL_GROUP_DIM ** -0.5 * nrm(ks[10], (DEPTH, N_POOL_GROUPS, POOL_GROUP_DIM, POOL_GROUP_DIM), f32)
    ls_pool = 1.0 + 0.1 * nrm(ks[11], (DEPTH, D_POOL), f32)
    mh_norm_g = 1.0 + 0.02 * nrm(ks[12], (DEPTH, D_MLSTM), f32)
    w_out = D_MIX ** -0.5 * nrm(ks[13], (DEPTH, D_MIX, D_MODEL), f32)
    final_g = 1.0 + 0.02 * nrm(ks[14], (D_MODEL,), f32)
    return {"x": x, "c": c, "norm_g": norm_g, "w_ada": w_ada, "b_ada": b_ada,
            "w_in": w_in, "b_gates": b_gates, "conv_w": conv_w, "conv_b": conv_b,
            "w_pool": w_pool, "ls_pool": ls_pool, "mh_norm_g": mh_norm_g,
            "w_out": w_out, "final_g": final_g}


def reference(x, c, norm_g, w_ada, b_ada, w_in, b_gates, conv_w, conv_b,
              w_pool, ls_pool, mh_norm_g, w_out, final_g):
    B, S, _ = x.shape
    for l in range(DEPTH):
        mod = jax.nn.silu(c) @ w_ada[l] + b_ada[l]
        shift, scale, gate = jnp.split(mod, 3, axis=-1)
        h = rms_norm(x, norm_g[l]) * (1.0 + scale[:, None, :]) + shift[:, None, :]
        proj = h @ w_in[l]

        u = proj[..., OFF_U:OFF_ZP]
        z_pool = proj[..., OFF_ZP:OFF_Q]
        pool_out = causal_pool_mixer(u, w_pool[l], ls_pool[l]) * jax.nn.silu(z_pool)

        qk = jax.nn.silu(causal_depthwise_conv(proj[..., OFF_Q:OFF_V], conv_w[l], conv_b[l]))
        q = qk[..., :D_MLSTM].reshape(B, S, N_HEADS, HEAD_DIM)
        k = qk[..., D_MLSTM:].reshape(B, S, N_HEADS, HEAD_DIM)
        v = proj[..., OFF_V:OFF_O].reshape(B, S, N_HEADS, HEAD_DIM)
        o_gate = jax.nn.sigmoid(proj[..., OFF_O:OFF_ZM])
        z_mlstm = proj[..., OFF_ZM:OFF_G]
        gates = (proj[..., OFF_G:] + b_gates[l]).astype(jnp.float32)
        ig = gates[..., :N_HEADS]
        lf = jax.nn.log_sigmoid(gates[..., N_HEADS:])
        hm = mlstm_chunkwise(q, k, v, ig, lf)
        hm = hm * lax.rsqrt(jnp.mean(hm * hm, axis=-1, keepdims=True) + EPS)
        hm = hm.reshape(B, S, D_MLSTM) * mh_norm_g[l].astype(jnp.float32)
        mlstm_out = hm.astype(x.dtype) * o_gate * jax.nn.silu(z_mlstm)

        mix = jnp.concatenate([pool_out, mlstm_out], axis=-1)
        x = x + gate[:, None, :] * (mix @ w_out[l])
    return rms_norm(x, final_g)
```

```python
import functools

import jax
import jax.numpy as jnp
from jax import lax
from jax.experimental import pallas as pl
from jax.experimental.pallas import tpu as pltpu

F32 = jnp.float32
BF16 = jnp.bfloat16

D_MODEL = 1024
D_POOL = 512
D_MLSTM = 512
POOL_WINDOWS = (2, 4, 8, 16)
POOL_GROUP_DIM = 128
N_HEADS = 4
HEAD_DIM = 128
CONV_WIDTH = 4
CHUNK = 128
EPS = 1e-6

LANES = 128
SUBLANES = 8
POOL_HIST = 16
CONV_HIST = SUBLANES
GATE_PAD = LANES

OFF_U = 0
OFF_ZP = 512
OFF_QK = 1024
OFF_V = 2048
OFF_O = 2560
OFF_ZM = 3072
OFF_G = 3584
D_PROJ_PAD = OFF_G + GATE_PAD

R_ZP = 0
R_V = 512
R_O = 1024
R_ZM = 1536
R_G = 2048
R_COLS = R_G + GATE_PAD

SEQ_TILE = 512
NORM_ROWS = 64
OUT_ROWS = 256
VMEM_LIMIT = 56 * 1024 * 1024


def _silu(z):
    return z * jax.nn.sigmoid(z)


def _log_sigmoid(z):
    return jnp.minimum(z, 0.0) - jnp.log1p(jnp.exp(-jnp.abs(z)))


def _lane_scan(x, op, fill):
    lane = lax.broadcasted_iota(jnp.int32, x.shape, 1)
    k = 1
    while k < x.shape[1]:
        shifted = pltpu.roll(x, k, axis=1)
        x = op(x, jnp.where(lane >= k, shifted, fill))
        k *= 2
    return x


def _ada_kernel(c_ref, w_ref, b_ref, o_ref):
    c = c_ref[...]
    o_ref[...] = jnp.dot(_silu(c), w_ref[...], preferred_element_type=F32,
                         precision=lax.Precision.HIGHEST) + b_ref[...]


def _block_kernel(x_ref, mod_ref, ng_ref, win_ref, bg_ref, cw_ref, cb_ref, wp_ref, ls_ref,
                  mg_ref, wout_ref, fg_ref, o_ref,
                  h_ref, u_ref, qk_ref, rest_ref, pooled_ref, mix_ref, caug_ref, m_ref):
    T = SEQ_TILE
    j = pl.program_id(1)

    @pl.when(j == 0)
    def _():
        u_ref[0:POOL_HIST, :] = jnp.zeros((POOL_HIST, D_POOL), F32)
        qk_ref[0:CONV_HIST, :] = jnp.zeros((CONV_HIST, 2 * D_MLSTM), F32)
        caug_ref[...] = jnp.zeros(caug_ref.shape, F32)
        m_ref[...] = jnp.zeros(m_ref.shape, F32)

    shift = mod_ref[0:1, :]
    gscale = ng_ref[...] * (1.0 + mod_ref[1:2, :])
    gate = mod_ref[2:3, :]

    def norm_body(i, carry):
        r = pl.multiple_of(i * NORM_ROWS, NORM_ROWS)
        x = x_ref[pl.ds(r, NORM_ROWS), :]
        ms = jnp.mean(x * x, axis=-1, keepdims=True)
        h_ref[pl.ds(r, NORM_ROWS), :] = (x * lax.rsqrt(ms + EPS) * gscale + shift).astype(BF16)
        return carry
    lax.fori_loop(0, T // NORM_ROWS, norm_body, 0)

    def proj(lo, hi):
        return jnp.dot(h_ref[...], win_ref[:, lo:hi], preferred_element_type=F32)
    u_ref[POOL_HIST:POOL_HIST + T, :] = proj(OFF_U, OFF_ZP)
    rest_ref[:, R_ZP:R_V] = proj(OFF_ZP, OFF_QK)
    qk_ref[CONV_HIST:CONV_HIST + T, :] = proj(OFF_QK, OFF_V)
    rest_ref[:, R_V:R_O] = proj(OFF_V, OFF_O)
    rest_ref[:, R_O:R_ZM] = proj(OFF_O, OFF_ZM)
    rest_ref[:, R_ZM:R_G] = proj(OFF_ZM, OFF_G)
    rest_ref[:, R_G:R_COLS] = proj(OFF_G, D_PROJ_PAD) + bg_ref[...]

    row_i = lax.broadcasted_iota(jnp.int32, (CHUNK, CHUNK), 0)
    col_i = lax.broadcasted_iota(jnp.int32, (CHUNK, CHUNK), 1)
    causal = col_i <= row_i
    ones_bf = jnp.ones((CHUNK, HEAD_DIM), BF16)
    zero_rows = jnp.zeros((CHUNK - 3 * SUBLANES, CHUNK), F32)

    for c in range(T // CHUNK):
        r0 = c * CHUNK

        for g, w in enumerate(POOL_WINDOWS):
            cs = slice(g * POOL_GROUP_DIM, (g + 1) * POOL_GROUP_DIM)
            cur = u_ref[POOL_HIST + r0:POOL_HIST + r0 + CHUNK, cs]
            acc = cur
            for k in range(1, w):
                acc = acc + u_ref[POOL_HIST + r0 - k:POOL_HIST + r0 - k + CHUNK, cs]
            pooled_ref[:, cs] = acc * (1.0 / w) - cur

        if c == 0:
            @pl.when(j == 0)
            def _():
                pos = lax.broadcasted_iota(jnp.int32, (POOL_HIST, POOL_GROUP_DIM), 0) + 1
                for g, w in enumerate(POOL_WINDOWS):
                    cs = slice(g * POOL_GROUP_DIM, (g + 1) * POOL_GROUP_DIM)
                    cur = u_ref[POOL_HIST:2 * POOL_HIST, cs]
                    fix = float(w) / jnp.minimum(pos, w).astype(F32)
                    pooled_ref[0:POOL_HIST, cs] = (pooled_ref[0:POOL_HIST, cs] + cur) * fix - cur

        for g in range(len(POOL_WINDOWS)):
            cs = slice(g * POOL_GROUP_DIM, (g + 1) * POOL_GROUP_DIM)
            y = jnp.dot(pooled_ref[:, cs].astype(BF16), wp_ref[g], preferred_element_type=F32)
            zp = rest_ref[r0:r0 + CHUNK, R_ZP + g * POOL_GROUP_DIM:R_ZP + (g + 1) * POOL_GROUP_DIM]
            mix_ref[r0:r0 + CHUNK, cs] = (y * ls_ref[:, cs] * _silu(zp)).astype(BF16)

        gt = rest_ref[r0:r0 + CHUNK, R_G:R_COLS].T[0:SUBLANES, :]
        lf = pltpu.roll(_log_sigmoid(gt), N_HEADS, axis=0)
        b = _lane_scan(lf, jnp.add, 0.0)
        a = gt - b
        m_old = m_ref[...]
        big_m = jnp.maximum(m_old, _lane_scan(a, jnp.maximum, -jnp.inf))
        inter = jnp.exp(m_old - big_m)
        efloor = jnp.exp(-(b + big_m))
        m_last = jnp.broadcast_to(big_m[:, CHUNK - 1:CHUNK], big_m.shape)
        w_row = jnp.exp(a - m_last)
        decay = jnp.exp(m_old - m_last)
        m_ref[...] = jnp.broadcast_to(b[:, CHUNK - 1:CHUNK], b.shape) + m_last
        cols = jnp.concatenate([big_m, inter, efloor, zero_rows], axis=0).T

        for hd in range(N_HEADS):
            hs = slice(hd * HEAD_DIM, (hd + 1) * HEAD_DIM)
            ks = slice(D_MLSTM + hd * HEAD_DIM, D_MLSTM + (hd + 1) * HEAD_DIM)

            def conv(cols_):
                y = cb_ref[:, cols_]
                for t in range(CONV_WIDTH):
                    lo = CONV_HIST + r0 - (CONV_WIDTH - 1) + t
                    y = y + cw_ref[t:t + 1, cols_] * qk_ref[lo:lo + CHUNK, cols_]
                return _silu(y)
            q = conv(hs)
            k = conv(ks) * (HEAD_DIM ** -0.5)
            v = rest_ref[r0:r0 + CHUNK, R_V + hd * HEAD_DIM:R_V + (hd + 1) * HEAD_DIM]

            m_col = cols[:, hd:hd + 1]
            inter_col = cols[:, SUBLANES + hd:SUBLANES + hd + 1]
            efloor_col = cols[:, 2 * SUBLANES + hd:2 * SUBLANES + hd + 1]

            kt = k.T
            s = jnp.dot(q.astype(BF16), kt.astype(BF16), preferred_element_type=F32)
            dmat = jnp.where(causal, jnp.exp(a[hd:hd + 1, :] - m_col), 0.0)
            p = (s * dmat).astype(BF16)
            qi = (q * inter_col).astype(BF16)
            vaug = jnp.concatenate([v.astype(BF16), ones_bf], axis=1)
            c_old = caug_ref[hd]
            numden = jnp.dot(jnp.concatenate([p, qi], axis=1),
                             jnp.concatenate([vaug, c_old.astype(BF16)], axis=0),
                             preferred_element_type=F32)
            num = numden[:, :HEAD_DIM]
            den = numden[:, HEAD_DIM:]
            hm = num / jnp.maximum(jnp.abs(den), efloor_col)

            kw = (kt * w_row[hd:hd + 1, :]).astype(BF16)
            dec = decay[hd:hd + 1, :]
            caug_ref[hd] = (jnp.concatenate([dec, dec], axis=1) * c_old
                            + jnp.dot(kw, vaug, preferred_element_type=F32))

            hm = hm * lax.rsqrt(jnp.mean(hm * hm, axis=-1, keepdims=True) + EPS) * mg_ref[:, hs]
            og = jax.nn.sigmoid(rest_ref[r0:r0 + CHUNK, R_O + hd * HEAD_DIM:R_O + (hd + 1) * HEAD_DIM])
            zm = rest_ref[r0:r0 + CHUNK, R_ZM + hd * HEAD_DIM:R_ZM + (hd + 1) * HEAD_DIM]
            mix_ref[r0:r0 + CHUNK, D_POOL + hd * HEAD_DIM:D_POOL + (hd + 1) * HEAD_DIM] = (
                hm * og * _silu(zm)).astype(BF16)

    u_ref[0:POOL_HIST, :] = u_ref[T:T + POOL_HIST, :]
    qk_ref[0:CONV_HIST, :] = qk_ref[T:T + CONV_HIST, :]

    fg = fg_ref[...]
    for rb in range(T // OUT_ROWS):
        rows = slice(rb * OUT_ROWS, (rb + 1) * OUT_ROWS)
        y = jnp.dot(mix_ref[rows, :], wout_ref[...], preferred_element_type=F32)
        z = x_ref[rows, :] + gate * y
        ms = jnp.mean(z * z, axis=-1, keepdims=True)
        o_ref[rows, :] = z * lax.rsqrt(ms + EPS) * fg


def _const_spec(shape):
    return pl.BlockSpec(shape, lambda b, j: (0,) * len(shape))


@jax.jit
def kernel(x, c, norm_g, w_ada, b_ada, w_in, b_gates, conv_w, conv_b, w_pool, ls_pool,
           mh_norm_g, w_out, final_g):
    B, S, D = x.shape
    assert D == D_MODEL and S % SEQ_TILE == 0
    assert w_in.shape == (1, D_MODEL, OFF_G + 2 * N_HEADS)

    n_ada = 3
    mod = pl.pallas_call(
        _ada_kernel,
        grid=(n_ada,),
        in_specs=[pl.BlockSpec((B, D), lambda n: (0, 0)),
                  pl.BlockSpec((D, D), lambda n: (0, n)),
                  pl.BlockSpec((1, D), lambda n: (0, n))],
        out_specs=pl.BlockSpec((B, D), lambda n: (0, n)),
        out_shape=jax.ShapeDtypeStruct((B, n_ada * D), F32),
    )(c, w_ada[0], b_ada)
    mod = mod.reshape(B, n_ada, D)

    pad = GATE_PAD - 2 * N_HEADS
    w_in_p = jnp.pad(w_in[0], ((0, 0), (0, pad))).astype(BF16)
    b_gates_p = jnp.pad(b_gates, ((0, 0), (0, pad)))

    T = SEQ_TILE
    out = pl.pallas_call(
        _block_kernel,
        grid=(B, S // T),
        in_specs=[
            pl.BlockSpec((None, T, D), lambda b, j: (b, j, 0)),
            pl.BlockSpec((None, n_ada, D), lambda b, j: (b, 0, 0)),
            _const_spec((1, D)),
            _const_spec((D, D_PROJ_PAD)),
            _const_spec((1, GATE_PAD)),
            _const_spec((CONV_WIDTH, 2 * D_MLSTM)),
            _const_spec((1, 2 * D_MLSTM)),
            _const_spec((len(POOL_WINDOWS), POOL_GROUP_DIM, POOL_GROUP_DIM)),
            _const_spec((1, D_POOL)),
            _const_spec((1, D_MLSTM)),
            _const_spec((D, D)),
            _const_spec((1, D)),
        ],
        out_specs=pl.BlockSpec((None, T, D), lambda b, j: (b, j, 0)),
        out_shape=jax.ShapeDtypeStruct((B, S, D), x.dtype),
        scratch_shapes=[
            pltpu.VMEM((T, D), BF16),
            pltpu.VMEM((POOL_HIST + T, D_POOL), F32),
            pltpu.VMEM((CONV_HIST + T, 2 * D_MLSTM), F32),
            pltpu.VMEM((T, R_COLS), F32),
            pltpu.VMEM((CHUNK, D_POOL), F32),
            pltpu.VMEM((T, D), BF16),
            pltpu.VMEM((N_HEADS, HEAD_DIM, 2 * HEAD_DIM), F32),
            pltpu.VMEM((SUBLANES, CHUNK), F32),
        ],
        compiler_params=pltpu.CompilerParams(
            dimension_semantics=("arbitrary", "arbitrary"),
            vmem_limit_bytes=VMEM_LIMIT),
    )(x, mod, norm_g, w_in_p, b_gates_p, conv_w[0], conv_b, w_pool[0].astype(BF16), ls_pool,
      mh_norm_g, w_out[0].astype(BF16), final_g.reshape(1, D))
    return out
```

```python
import jax
import jax.numpy as jnp
from jax import lax
from jax.experimental import pallas as pl
from jax.experimental.pallas import tpu as pltpu

F32 = jnp.float32
BF16 = jnp.bfloat16

D_MODEL = 1024
D_POOL = 512
D_MLSTM = 512
POOL_WINDOWS = (2, 4, 8, 16)
POOL_GROUP_DIM = 128
N_HEADS = 4
HEAD_DIM = 128
CONV_WIDTH = 4
CHUNK = 128
EPS = 1e-6

LANES = 128
SUBLANES = 8
MXU_COLS = 256
POOL_HIST = 16
CONV_HIST = SUBLANES
GATE_PAD = LANES

OFF_U = 0
OFF_ZP = 512
OFF_QK = 1024
OFF_V = 2048
OFF_G = 3584
D_PROJ_PAD = OFF_G + GATE_PAD

R_ZP = 0
R_V = 512
R_O = 1024
R_ZM = 1536
R_G = 2048
R_COLS = R_G + GATE_PAD

SEQ_TILE = 512
N_CHUNKS = SEQ_TILE // CHUNK
NORM_ROWS = 64
OUT_ROWS = 256
PROJ_ROWS = 256
PIPE_DEPTH = 3
VMEM_LIMIT = 56 * 1024 * 1024


def _silu(z):
    return z * jax.nn.sigmoid(z)


def _log_sigmoid(z):
    return jnp.minimum(z, 0.0) - jnp.log1p(jnp.exp(-jnp.abs(z)))


def _lane_scan(x, op, fill):
    lane = lax.broadcasted_iota(jnp.int32, x.shape, 1)
    k = 1
    while k < x.shape[1]:
        shifted = pltpu.roll(x, k, axis=1)
        x = op(x, jnp.where(lane >= k, shifted, fill))
        k *= 2
    return x


def _ada_kernel(c_ref, w_ref, b_ref, o_ref):
    c = c_ref[...]
    o_ref[...] = jnp.dot(_silu(c), w_ref[...], preferred_element_type=F32,
                         precision=lax.Precision.HIGHEST) + b_ref[...]


def _proj_pieces():
    pieces = []
    for lo in range(OFF_U, OFF_ZP, MXU_COLS):
        pieces.append((lo, MXU_COLS, "u", lo - OFF_U))
    for lo in range(OFF_ZP, OFF_QK, MXU_COLS):
        pieces.append((lo, MXU_COLS, "rest", R_ZP + lo - OFF_ZP))
    for lo in range(OFF_QK, OFF_V, MXU_COLS):
        pieces.append((lo, MXU_COLS, "qk", lo - OFF_QK))
    for lo in range(OFF_V, OFF_G, MXU_COLS):
        pieces.append((lo, MXU_COLS, "rest", R_V + lo - OFF_V))
    pieces.append((OFF_G, GATE_PAD, "gates", R_G))
    return pieces


def _make_block_kernel(tiles_per_seq, n_tiles):
    T = SEQ_TILE

    def block_kernel(x_ref, xres_ref, mod_ref, ng_ref, win_ref, bg_ref, cw_ref, cb_ref, wp_ref,
                     ls_ref, mg_ref, wout_ref, fg_ref, o_ref,
                     h_ref, u_ref, qk_ref, rest_ref, pooled_ref, mix_ref, caug_ref, m_ref):
        s = pl.program_id(0)
        tile_n = jnp.minimum(s, n_tiles - 1)
        tile_c = jnp.clip(s - (PIPE_DEPTH - 1), 0, n_tiles - 1)
        pn = s % 2
        pw = s % 2
        pr = 1 - pw

        @pl.when(s == 0)
        def _():
            h_ref[...] = jnp.zeros(h_ref.shape, BF16)
            u_ref[...] = jnp.zeros(u_ref.shape, F32)
            qk_ref[...] = jnp.zeros(qk_ref.shape, F32)
            rest_ref[...] = jnp.zeros(rest_ref.shape, F32)
            caug_ref[...] = jnp.zeros(caug_ref.shape, F32)
            m_ref[...] = jnp.zeros(m_ref.shape, F32)

        first_of_seq = (s - (PIPE_DEPTH - 1)) % tiles_per_seq == 0

        @pl.when(first_of_seq)
        def _():
            u_ref[pr, 0:POOL_HIST, :] = jnp.zeros((POOL_HIST, D_POOL), F32)
            qk_ref[pr, 0:CONV_HIST, :] = jnp.zeros((CONV_HIST, 2 * D_MLSTM), F32)
            caug_ref[...] = jnp.zeros(caug_ref.shape, F32)
            m_ref[...] = jnp.zeros(m_ref.shape, F32)

        mod_n = mod_ref[tile_n // tiles_per_seq]
        shift = mod_n[0:1, :]
        gscale = ng_ref[...] * (1.0 + mod_n[1:2, :])
        gate = mod_ref[tile_c // tiles_per_seq][2:3, :]

        def norm_block(i):
            rows = slice(i * NORM_ROWS, (i + 1) * NORM_ROWS)
            x = x_ref[rows, :]
            ms = jnp.mean(x * x, axis=-1, keepdims=True)
            h_ref[pn, rows, :] = (x * lax.rsqrt(ms + EPS) * gscale + shift).astype(BF16)

        def proj_piece(piece, rb):
            lo, width, dst, col = piece
            r0, r1 = rb * PROJ_ROWS, (rb + 1) * PROJ_ROWS
            y = jnp.dot(h_ref[1 - pn, r0:r1, :], win_ref[:, lo:lo + width],
                        preferred_element_type=F32)
            if dst == "u":
                u_ref[pw, POOL_HIST + r0:POOL_HIST + r1, col:col + width] = y
            elif dst == "qk":
                qk_ref[pw, CONV_HIST + r0:CONV_HIST + r1, col:col + width] = y
            elif dst == "rest":
                rest_ref[pw, r0:r1, col:col + width] = y
            else:
                rest_ref[pw, r0:r1, col:col + width] = y + bg_ref[...]

        row_i = lax.broadcasted_iota(jnp.int32, (CHUNK, CHUNK), 0)
        col_i = lax.broadcasted_iota(jnp.int32, (CHUNK, CHUNK), 1)
        causal = col_i <= row_i
        ones_bf = jnp.ones((CHUNK, HEAD_DIM), BF16)
        zero_rows = jnp.zeros((CHUNK - 3 * SUBLANES, CHUNK), F32)

        def pool_chunk(c):
            r0 = c * CHUNK
            for g, w in enumerate(POOL_WINDOWS):
                cs = slice(g * POOL_GROUP_DIM, (g + 1) * POOL_GROUP_DIM)
                cur = u_ref[pr, POOL_HIST + r0:POOL_HIST + r0 + CHUNK, cs]
                acc = cur
                for k in range(1, w):
                    acc = acc + u_ref[pr, POOL_HIST + r0 - k:POOL_HIST + r0 - k + CHUNK, cs]
                pooled_ref[:, cs] = acc * (1.0 / w) - cur
                if c == 0:
                    pos = lax.broadcasted_iota(jnp.int32, (POOL_HIST, POOL_GROUP_DIM), 0) + 1
                    short = 1.0 / jnp.minimum(pos, w).astype(F32) - 1.0 / w
                    scale = 1.0 / w + first_of_seq.astype(F32) * short
                    pooled_ref[0:POOL_HIST, cs] = acc[0:POOL_HIST] * scale - cur[0:POOL_HIST]

            yield
            for g in range(len(POOL_WINDOWS)):
                cs = slice(g * POOL_GROUP_DIM, (g + 1) * POOL_GROUP_DIM)
                y = jnp.dot(pooled_ref[:, cs].astype(BF16), wp_ref[g], preferred_element_type=F32)
                zp = rest_ref[pr, r0:r0 + CHUNK, R_ZP + g * POOL_GROUP_DIM:R_ZP + (g + 1) * POOL_GROUP_DIM]
                mix_ref[r0:r0 + CHUNK, cs] = (y * ls_ref[:, cs] * _silu(zp)).astype(BF16)

        def gates_chunk(c):
            r0 = c * CHUNK
            gt = rest_ref[pr, r0:r0 + CHUNK, R_G:R_COLS].T[0:SUBLANES, :]
            lf = pltpu.roll(_log_sigmoid(gt), N_HEADS, axis=0)
            b = _lane_scan(lf, jnp.add, 0.0)
            a = gt - b
            return a, b, _lane_scan(a, jnp.maximum, -jnp.inf)

        def state_chunk(a, b, cmax):
            m_old = m_ref[...]
            big_m = jnp.maximum(m_old, cmax)
            inter = jnp.exp(m_old - big_m)
            efloor = jnp.exp(-(b + big_m))
            m_last = jnp.broadcast_to(big_m[:, CHUNK - 1:CHUNK], big_m.shape)
            w_row = jnp.exp(a - m_last)
            decay = jnp.exp(m_old - m_last)
            m_ref[...] = jnp.broadcast_to(b[:, CHUNK - 1:CHUNK], b.shape) + m_last
            cols = jnp.concatenate([big_m, inter, efloor, zero_rows], axis=0).T
            return cols, w_row, decay

        def head_chunk(c, hd, a, cols, w_row, decay):
            r0 = c * CHUNK
            hs = slice(hd * HEAD_DIM, (hd + 1) * HEAD_DIM)
            ks = slice(D_MLSTM + hd * HEAD_DIM, D_MLSTM + (hd + 1) * HEAD_DIM)

            def conv(cols_):
                y = cb_ref[:, cols_]
                for t in range(CONV_WIDTH):
                    lo = CONV_HIST + r0 - (CONV_WIDTH - 1) + t
                    y = y + cw_ref[t:t + 1, cols_] * qk_ref[pr, lo:lo + CHUNK, cols_]
                return _silu(y)
            q = conv(hs)
            k = conv(ks) * (HEAD_DIM ** -0.5)
            v = rest_ref[pr, r0:r0 + CHUNK, R_V + hd * HEAD_DIM:R_V + (hd + 1) * HEAD_DIM]

            m_col = cols[:, hd:hd + 1]
            inter_col = cols[:, SUBLANES + hd:SUBLANES + hd + 1]
            efloor_col = cols[:, 2 * SUBLANES + hd:2 * SUBLANES + hd + 1]

            kt = k.T
            q_bf = q.astype(BF16)
            kt_bf = kt.astype(BF16)
            kw = (kt * w_row[hd:hd + 1, :]).astype(BF16)
            vaug = jnp.concatenate([v.astype(BF16), ones_bf], axis=1)
            c_old = caug_ref[hd]
            rhs = jnp.concatenate([vaug, c_old.astype(BF16)], axis=0)
            qi = (q * inter_col).astype(BF16)
            dmat = jnp.where(causal, jnp.exp(a[hd:hd + 1, :] - m_col), 0.0)
            dec = decay[hd:hd + 1, :]
            c_dec = jnp.concatenate([dec, dec], axis=1) * c_old

            yield
            sc = jnp.dot(q_bf, kt_bf, preferred_element_type=F32)
            caug_ref[hd] = c_dec + jnp.dot(kw, vaug, preferred_element_type=F32)
            p = (sc * dmat).astype(BF16)

            yield
            numden = jnp.dot(jnp.concatenate([p, qi], axis=1), rhs, preferred_element_type=F32)
            num = numden[:, :HEAD_DIM]
            den = numden[:, HEAD_DIM:]
            hm = num / jnp.maximum(jnp.abs(den), efloor_col)

            hm = hm * lax.rsqrt(jnp.mean(hm * hm, axis=-1, keepdims=True) + EPS) * mg_ref[:, hs]
            og = jax.nn.sigmoid(rest_ref[pr, r0:r0 + CHUNK, R_O + hd * HEAD_DIM:R_O + (hd + 1) * HEAD_DIM])
            zm = rest_ref[pr, r0:r0 + CHUNK, R_ZM + hd * HEAD_DIM:R_ZM + (hd + 1) * HEAD_DIM]
            mix_ref[r0:r0 + CHUNK, D_POOL + hd * HEAD_DIM:D_POOL + (hd + 1) * HEAD_DIM] = (
                hm * og * _silu(zm)).astype(BF16)

        def out_block(rb):
            rows = slice(rb * OUT_ROWS, (rb + 1) * OUT_ROWS)
            y = jnp.dot(mix_ref[rows, :], wout_ref[...], preferred_element_type=F32)
            z = xres_ref[rows, :] + gate * y
            ms = jnp.mean(z * z, axis=-1, keepdims=True)
            o_ref[rows, :] = z * lax.rsqrt(ms + EPS) * fg_ref[...]

        fill = [lambda p=p, rb=rb: proj_piece(p, rb)
                for p in _proj_pieces() for rb in range(T // PROJ_ROWS)]
        norm_fill = [lambda i=i: norm_block(i) for i in range(T // NORM_ROWS)]

        def emit(work, n):
            for _ in range(n):
                if work:
                    work.pop(0)()

        lead = 2
        n_yields = N_CHUNKS * (1 + 2 * N_HEADS)
        n_spread = len(fill) - lead
        seen = [0]

        def run(gen):
            for _ in gen:
                seen[0] += 1
                emit(fill, (seen[0] * n_spread) // n_yields - ((seen[0] - 1) * n_spread) // n_yields)

        emit(fill, lead)
        scans = [gates_chunk(c) for c in range(N_CHUNKS)]
        for c in range(N_CHUNKS):
            a, b, cmax = scans[c]
            cols, w_row, decay = state_chunk(a, b, cmax)
            run(pool_chunk(c))
            for hd in range(N_HEADS):
                run(head_chunk(c, hd, a, cols, w_row, decay))
        emit(fill, len(fill))

        u_ref[pw, 0:POOL_HIST, :] = u_ref[pr, T:T + POOL_HIST, :]
        qk_ref[pw, 0:CONV_HIST, :] = qk_ref[pr, T:T + CONV_HIST, :]

        for rb in range(T // OUT_ROWS):
            emit(norm_fill, len(norm_fill) // (T // OUT_ROWS))
            out_block(rb)
        emit(norm_fill, len(norm_fill))

    return block_kernel


def _const_spec(shape):
    return pl.BlockSpec(shape, lambda s: (0,) * len(shape), pipeline_mode=pl.Buffered(1))


@jax.jit
def kernel(x, c, norm_g, w_ada, b_ada, w_in, b_gates, conv_w, conv_b, w_pool, ls_pool,
           mh_norm_g, w_out, final_g):
    B, S, D = x.shape
    T = SEQ_TILE
    assert D == D_MODEL and S % T == 0
    assert w_in.shape == (1, D_MODEL, OFF_G + 2 * N_HEADS)
    tiles_per_seq = S // T
    n_tiles = B * tiles_per_seq

    n_ada = 3
    mod = pl.pallas_call(
        _ada_kernel,
        grid=(n_ada,),
        in_specs=[pl.BlockSpec((B, D), lambda n: (0, 0)),
                  pl.BlockSpec((D, D), lambda n: (0, n)),
                  pl.BlockSpec((1, D), lambda n: (0, n))],
        out_specs=pl.BlockSpec((B, D), lambda n: (0, n)),
        out_shape=jax.ShapeDtypeStruct((B, n_ada * D), F32),
    )(c, w_ada[0], b_ada)
    mod = mod.reshape(B, n_ada, D)

    pad = GATE_PAD - 2 * N_HEADS
    w_in_p = jnp.pad(w_in[0], ((0, 0), (0, pad))).astype(BF16)
    b_gates_p = jnp.pad(b_gates, ((0, 0), (0, pad)))
    x_tiles = x.reshape(n_tiles, T, D)

    def tile_n(s):
        return (jnp.minimum(s, n_tiles - 1), 0, 0)

    def tile_c(s):
        return (jnp.clip(s - (PIPE_DEPTH - 1), 0, n_tiles - 1), 0, 0)

    out = pl.pallas_call(
        _make_block_kernel(tiles_per_seq, n_tiles),
        grid=(n_tiles + PIPE_DEPTH - 1,),
        in_specs=[
            pl.BlockSpec((None, T, D), tile_n),
            pl.BlockSpec((None, T, D), tile_c),
            _const_spec((B, n_ada, D)),
            _const_spec((1, D)),
            _const_spec((D, D_PROJ_PAD)),
            _const_spec((1, GATE_PAD)),
            _const_spec((CONV_WIDTH, 2 * D_MLSTM)),
            _const_spec((1, 2 * D_MLSTM)),
            _const_spec((len(POOL_WINDOWS), POOL_GROUP_DIM, POOL_GROUP_DIM)),
            _const_spec((1, D_POOL)),
            _const_spec((1, D_MLSTM)),
            _const_spec((D, D)),
            _const_spec((1, D)),
        ],
        out_specs=pl.BlockSpec((None, T, D), tile_c),
        out_shape=jax.ShapeDtypeStruct((n_tiles, T, D), x.dtype),
        scratch_shapes=[
            pltpu.VMEM((2, T, D), BF16),
            pltpu.VMEM((2, POOL_HIST + T, D_POOL), F32),
            pltpu.VMEM((2, CONV_HIST + T, 2 * D_MLSTM), F32),
            pltpu.VMEM((2, T, R_COLS), F32),
            pltpu.VMEM((CHUNK, D_POOL), F32),
            pltpu.VMEM((T, D), BF16),
            pltpu.VMEM((N_HEADS, HEAD_DIM, 2 * HEAD_DIM), F32),
            pltpu.VMEM((SUBLANES, CHUNK), F32),
        ],
        compiler_params=pltpu.CompilerParams(
            dimension_semantics=("arbitrary",),
            vmem_limit_bytes=VMEM_LIMIT),
    )(x_tiles, x_tiles, mod, norm_g, w_in_p, b_gates_p, conv_w[0], conv_b,
      w_pool[0].astype(BF16), ls_pool, mh_norm_g, w_out[0].astype(BF16), final_g.reshape(1, D))
    return out.reshape(B, S, D)
```

```python
import jax
import jax.numpy as jnp
from jax import lax
from jax.experimental import pallas as pl
from jax.experimental.pallas import tpu as pltpu

F32 = jnp.float32
BF16 = jnp.bfloat16

D_MODEL = 1024
D_POOL = 512
D_MLSTM = 512
POOL_WINDOWS = (2, 4, 8, 16)
POOL_GROUP_DIM = 128
N_HEADS = 4
HEAD_DIM = 128
CONV_WIDTH = 4
CHUNK = 128
EPS = 1e-6

LANES = 128
SUBLANES = 8
MXU_COLS = 256
POOL_HIST = CHUNK
POOL_FIX_ROWS = 16
CONV_HIST = SUBLANES
GATE_PAD = LANES

OFF_U = 0
OFF_ZP = 512
OFF_QK = 1024
OFF_V = 2048
OFF_G = 3584
D_PROJ_PAD = OFF_G + GATE_PAD

R_ZP = 0
R_V = 512
R_O = 1024
R_ZM = 1536
R_COLS = 2048

G_A = 0
G_W = SUBLANES
G_DECAY = 2 * SUBLANES
G_ROWS = 3 * SUBLANES

GC_M = 0
GC_INTER = SUBLANES
GC_EFLOOR = 2 * SUBLANES

SEQ_TILE = 512
N_CHUNKS = SEQ_TILE // CHUNK
STRIP = 32
NORM_STRIP = 16
NORM_ROWS = 64
OUT_ROWS = 256
PROJ_ROWS = 256
PROJ_COLS = 2 * MXU_COLS
PIPE_DEPTH = 3
N_PAIRED = 6
N_HEAD_UNITS = N_CHUNKS * N_HEADS
IN_FLIGHT = 3
ROUNDS_PER_TILE = 44
VMEM_LIMIT = 56 * 1024 * 1024


def _silu(z):
    return z * jax.nn.sigmoid(z)


def _log_sigmoid(z):
    return jnp.minimum(z, 0.0) - jnp.log1p(jnp.exp(-jnp.abs(z)))


def _strips(n, step):
    return [slice(r, r + step) for r in range(0, n, step)]


def _ada_kernel(c_ref, w_ref, b_ref, o_ref):
    c = c_ref[...]
    o_ref[...] = jnp.dot(_silu(c), w_ref[...], preferred_element_type=F32,
                         precision=lax.Precision.HIGHEST) + b_ref[...]


def _proj_pieces():
    pieces = []
    for lo in range(OFF_U, OFF_ZP, PROJ_COLS):
        pieces.append((lo, PROJ_COLS, "u", lo - OFF_U))
    for lo in range(OFF_ZP, OFF_QK, PROJ_COLS):
        pieces.append((lo, PROJ_COLS, "rest", R_ZP + lo - OFF_ZP))
    for lo in range(OFF_QK, OFF_V, PROJ_COLS):
        pieces.append((lo, PROJ_COLS, "qk", lo - OFF_QK))
    for lo in range(OFF_V, OFF_G, PROJ_COLS):
        pieces.append((lo, PROJ_COLS, "rest", R_V + lo - OFF_V))
    return pieces


def _make_block_kernel(tiles_per_seq, n_tiles):
    T = SEQ_TILE

    def block_kernel(x_ref, xres_ref, mod_ref, ng_ref, win_ref, bg_ref, cw_ref, cb_ref, wp_ref,
                     ls_ref, mg_ref, wout_ref, fg_ref, o_ref, *scratch):
        s = pl.program_id(0)
        pairs, shared = scratch[:2 * N_PAIRED], scratch[2 * N_PAIRED:]
        band_ref, caug_ref, m_ref, rhs2_ref = shared[0], shared[2], shared[3], shared[-1]

        @pl.when(s == 0)
        def _():
            for ref in pairs + (caug_ref, m_ref):
                ref[...] = jnp.zeros(ref.shape, ref.dtype)
            rhs2_ref[:, 0:CHUNK, HEAD_DIM:] = jnp.ones((N_HEAD_UNITS, CHUNK, HEAD_DIM), BF16)
            t_i = lax.broadcasted_iota(jnp.int32, (CHUNK, 2 * CHUNK), 0)
            r_i = lax.broadcasted_iota(jnp.int32, (CHUNK, 2 * CHUNK), 1) - POOL_HIST
            for g, w in enumerate(POOL_WINDOWS):
                in_window = (r_i <= t_i) & (r_i > t_i - w)
                band = jnp.where(in_window, 1.0 / w, 0.0) - jnp.where(r_i == t_i, 1.0, 0.0)
                band_ref[g] = band.astype(BF16)

        inputs = (x_ref, xres_ref, mod_ref, ng_ref, win_ref, bg_ref, cw_ref, cb_ref, wp_ref,
                  ls_ref, mg_ref, wout_ref, fg_ref, o_ref)
        for parity in range(2):
            written = pairs[parity::2]
            read = pairs[1 - parity::2]
            bufs = tuple(r for wr in zip(written, read) for r in wr)

            @pl.when(s % 2 == parity)
            def _(bufs=bufs):
                step(s, *inputs, *bufs, *shared)

    def step(s, x_ref, xres_ref, mod_ref, ng_ref, win_ref, bg_ref, cw_ref, cb_ref, wp_ref,
             ls_ref, mg_ref, wout_ref, fg_ref, o_ref,
             h_w, h_r, u_w, u_r, qk_w, qk_r, rest_w, rest_r, gcol_w, gcol_r, grow_w, grow_r,
             band_ref, mix_ref, caug_ref, m_ref, pooled_ref, kf_ref, ktbf_ref, kwt_ref, qbf_ref,
             lhs2_ref, rhs2_ref):
        tile_n = jnp.minimum(s, n_tiles - 1)
        tile_c = jnp.clip(s - (PIPE_DEPTH - 1), 0, n_tiles - 1)

        first_of_seq = (s - (PIPE_DEPTH - 1)) % tiles_per_seq == 0
        first_of_seq_g = (s - 1) % tiles_per_seq == 0

        @pl.when(first_of_seq)
        def _():
            u_r[0:POOL_HIST, :] = jnp.zeros((POOL_HIST, D_POOL), BF16)
            qk_r[0:CONV_HIST, :] = jnp.zeros((CONV_HIST, 2 * D_MLSTM), F32)
            caug_ref[...] = jnp.zeros(caug_ref.shape, F32)

        mod_n = mod_ref[tile_n // tiles_per_seq]
        shift = mod_n[0:1, :]
        gscale = ng_ref[...] * (1.0 + mod_n[1:2, :])
        gate = mod_ref[tile_c // tiles_per_seq][2:3, :]

        def norm_block(i):
            for rs in _strips(NORM_ROWS, NORM_STRIP):
                rows = slice(i * NORM_ROWS + rs.start, i * NORM_ROWS + rs.stop)
                x = x_ref[rows, :]
                ms = jnp.mean(x * x, axis=-1, keepdims=True)
                h_w[rows, :] = (x * lax.rsqrt(ms + EPS) * gscale + shift).astype(BF16)

        def proj_piece(piece, rb):
            lo, width, dst, col = piece
            r0, r1 = rb * PROJ_ROWS, (rb + 1) * PROJ_ROWS
            y = jnp.dot(h_r[r0:r1, :], win_ref[:, lo:lo + width],
                        preferred_element_type=F32)
            if dst == "u":
                u_w[POOL_HIST + r0:POOL_HIST + r1, col:col + width] = y.astype(BF16)
            elif dst == "qk":
                qk_w[CONV_HIST + r0:CONV_HIST + r1, col:col + width] = y
            else:
                rest_w[r0:r1, col:col + width] = y

        zero_rows = jnp.zeros((CHUNK - 3 * SUBLANES, CHUNK), F32)

        def gates_tile():
            gts = []
            for rb in range(T // PROJ_ROWS):
                g_pre = jnp.dot(h_r[rb * PROJ_ROWS:(rb + 1) * PROJ_ROWS, :],
                                win_ref[:, OFF_G:D_PROJ_PAD],
                                preferred_element_type=F32) + bg_ref[...]
                for cc in range(PROJ_ROWS // CHUNK):
                    gts.append(g_pre[cc * CHUNK:(cc + 1) * CHUNK, :].T[0:SUBLANES, :])
            yield
            lane = lax.broadcasted_iota(jnp.int32, (SUBLANES, CHUNK), 1)
            bs = [pltpu.roll(_log_sigmoid(gt), N_HEADS, axis=0) for gt in gts]
            k = 1
            while k < CHUNK:
                bs = [x + jnp.where(lane >= k, pltpu.roll(x, k, axis=1), 0.0) for x in bs]
                k *= 2
                yield
            a_s = [gt - b for gt, b in zip(gts, bs)]
            cm = a_s
            k = 1
            while k < CHUNK:
                cm = [jnp.maximum(x, jnp.where(lane >= k, pltpu.roll(x, k, axis=1), -jnp.inf))
                      for x in cm]
                k *= 2
                yield
            m_run = jnp.where(first_of_seq_g, 0.0, m_ref[...])
            for c in range(N_CHUNKS):
                a, b = a_s[c], bs[c]
                big_m = jnp.maximum(m_run, cm[c])
                m_last = jnp.broadcast_to(big_m[:, CHUNK - 1:CHUNK], big_m.shape)
                b_last = jnp.broadcast_to(b[:, CHUNK - 1:CHUNK], b.shape)
                yield
                inter = jnp.exp(m_run - big_m)
                efloor = jnp.exp(-(b + big_m))
                grow_w[c, G_A:G_A + SUBLANES, :] = a
                grow_w[c, G_W:G_W + SUBLANES, :] = jnp.exp(a - m_last)
                grow_w[c, G_DECAY:G_DECAY + SUBLANES, :] = jnp.exp(m_run - m_last)
                gcol_w[c] = jnp.concatenate([big_m, inter, efloor, zero_rows], axis=0).T
                m_run = b_last + m_last
            m_ref[...] = m_run

        def pool_chunk(c, slot):
            r0 = c * CHUNK
            for g, w in enumerate(POOL_WINDOWS):
                cs = slice(g * POOL_GROUP_DIM, (g + 1) * POOL_GROUP_DIM)
                pg = jnp.dot(band_ref[g], u_r[r0:r0 + 2 * CHUNK, cs],
                             preferred_element_type=F32)
                if c == 0:
                    cur = u_r[POOL_HIST:POOL_HIST + POOL_FIX_ROWS, cs].astype(F32)
                    pos = lax.broadcasted_iota(jnp.int32, (POOL_FIX_ROWS, POOL_GROUP_DIM), 0) + 1
                    short = float(w) / jnp.minimum(pos, w).astype(F32) - 1.0
                    scale = 1.0 + first_of_seq.astype(F32) * short
                    fixed = (pg[0:POOL_FIX_ROWS] + cur) * scale - cur
                    pooled_ref[slot, 0:POOL_FIX_ROWS, cs] = fixed.astype(BF16)
                    pooled_ref[slot, POOL_FIX_ROWS:, cs] = pg[POOL_FIX_ROWS:].astype(BF16)
                else:
                    pooled_ref[slot, :, cs] = pg.astype(BF16)
            yield
            for pair in range(len(POOL_WINDOWS) // 2):
                cs = slice(pair * MXU_COLS, (pair + 1) * MXU_COLS)
                y = jnp.dot(pooled_ref[slot, :, cs], wp_ref[pair], preferred_element_type=F32)
                for rs in _strips(CHUNK, STRIP):
                    rows = slice(r0 + rs.start, r0 + rs.stop)
                    zp = rest_r[rows, R_ZP + pair * MXU_COLS:R_ZP + (pair + 1) * MXU_COLS]
                    mix_ref[rows, cs] = (y[rs] * ls_ref[:, cs] * _silu(zp)).astype(BF16)

        def head_chunk(c, hd, slot):
            r0 = c * CHUNK
            hs = slice(hd * HEAD_DIM, (hd + 1) * HEAD_DIM)
            ks = slice(D_MLSTM + hd * HEAD_DIM, D_MLSTM + (hd + 1) * HEAD_DIM)

            def conv(cols_, rs):
                y = cb_ref[:, cols_]
                for t in range(CONV_WIDTH):
                    lo = CONV_HIST + r0 + rs.start - (CONV_WIDTH - 1) + t
                    y = y + cw_ref[t:t + 1, cols_] * qk_r[lo:lo + STRIP, cols_]
                return _silu(y)

            def gcol(col, rs):
                return gcol_r[c, rs, col + hd:col + hd + 1]

            for rs in _strips(CHUNK, STRIP):
                kf_ref[slot, rs, :] = conv(ks, rs) * (HEAD_DIM ** -0.5)
            yield
            kt = kf_ref[slot].T
            w_row = grow_r[c, G_W + hd:G_W + hd + 1, :]
            for rs in _strips(HEAD_DIM, STRIP):
                ktbf_ref[slot, rs, :] = kt[rs].astype(BF16)
                kwt_ref[slot, rs, :] = (kt[rs] * w_row).astype(BF16)
                rhs2_ref[slot, CHUNK + rs.start:CHUNK + rs.stop, :] = caug_ref[hd, rs, :].astype(BF16)
            yield
            for rs in _strips(CHUNK, STRIP):
                rows = slice(r0 + rs.start, r0 + rs.stop)
                q = conv(hs, rs)
                qbf_ref[slot, rs, :] = q.astype(BF16)
                lhs2_ref[slot, rs, CHUNK:] = (q * gcol(GC_INTER, rs)).astype(BF16)
                v = rest_r[rows, R_V + hd * HEAD_DIM:R_V + (hd + 1) * HEAD_DIM]
                rhs2_ref[slot, rs, 0:HEAD_DIM] = v.astype(BF16)
            yield
            sc = jnp.dot(qbf_ref[slot], ktbf_ref[slot], preferred_element_type=F32)
            upd = jnp.dot(kwt_ref[slot], rhs2_ref[slot, 0:CHUNK, :], preferred_element_type=F32)
            dec = grow_r[c, G_DECAY + hd:G_DECAY + hd + 1, :]
            dec2 = jnp.concatenate([dec, dec], axis=1)
            for rs in _strips(HEAD_DIM, STRIP):
                caug_ref[hd, rs, :] = dec2 * caug_ref[hd, rs, :] + upd[rs]
            yield
            a_row = grow_r[c, G_A + hd:G_A + hd + 1, :]
            for rs in _strips(CHUNK, STRIP):
                row_i = lax.broadcasted_iota(jnp.int32, (STRIP, CHUNK), 0) + rs.start
                col_i = lax.broadcasted_iota(jnp.int32, (STRIP, CHUNK), 1)
                dmat = jnp.where(col_i <= row_i, jnp.exp(a_row - gcol(GC_M, rs)), 0.0)
                lhs2_ref[slot, rs, 0:CHUNK] = (sc[rs] * dmat).astype(BF16)
            yield
            numden = jnp.dot(lhs2_ref[slot], rhs2_ref[slot], preferred_element_type=F32)
            yield
            for rs in _strips(CHUNK, STRIP):
                rows = slice(r0 + rs.start, r0 + rs.stop)
                num = numden[rs, :HEAD_DIM]
                den = numden[rs, HEAD_DIM:]
                hm = num / jnp.maximum(jnp.abs(den), gcol(GC_EFLOOR, rs))
                ms = jnp.mean(hm * hm, axis=-1, keepdims=True)
                og = jax.nn.sigmoid(rest_r[rows, R_O + hd * HEAD_DIM:R_O + (hd + 1) * HEAD_DIM])
                zm = rest_r[rows, R_ZM + hd * HEAD_DIM:R_ZM + (hd + 1) * HEAD_DIM]
                hm = hm * lax.rsqrt(ms + EPS) * mg_ref[:, hs]
                mix_ref[rows, D_POOL + hd * HEAD_DIM:D_POOL + (hd + 1) * HEAD_DIM] = (
                    hm * og * _silu(zm)).astype(BF16)

        def out_block(rb):
            rows = slice(rb * OUT_ROWS, (rb + 1) * OUT_ROWS)
            y = jnp.dot(mix_ref[rows, :], wout_ref[...], preferred_element_type=F32)
            o_ref[rows, :] = xres_ref[rows, :] + gate * y
            for rs in _strips(OUT_ROWS, NORM_STRIP):
                rows = slice(rb * OUT_ROWS + rs.start, rb * OUT_ROWS + rs.stop)
                z = o_ref[rows, :]
                ms = jnp.mean(z * z, axis=-1, keepdims=True)
                o_ref[rows, :] = z * lax.rsqrt(ms + EPS) * fg_ref[...]

        fill = [lambda p=p, rb=rb: proj_piece(p, rb)
                for p in _proj_pieces() for rb in range(T // PROJ_ROWS)]
        norm_fill = [lambda i=i: norm_block(i) for i in range(T // NORM_ROWS)]

        def emit(work, n):
            for _ in range(n):
                if work:
                    work.pop(0)()

        def advance(gen):
            try:
                next(gen)
                return True
            except StopIteration:
                return False

        units = []
        for c in range(N_CHUNKS):
            units.append(pool_chunk(c, c))
            units.extend(head_chunk(c, hd, c * N_HEADS + hd) for hd in range(N_HEADS))
        n_rounds = ROUNDS_PER_TILE
        len_fill = len(fill)
        gates = gates_tile()
        advance(gates)
        active = []
        rounds = 0
        while units or active:
            while units and len(active) < IN_FLIGHT:
                active.append(units.pop(0))
            active = [gen for gen in active if advance(gen)]
            rounds += 1
            emit(fill, (rounds * len_fill) // n_rounds - ((rounds - 1) * len_fill) // n_rounds)
            advance(gates)
        for _ in gates:
            pass
        emit(fill, len(fill))

        u_w[0:POOL_HIST, :] = u_r[T:T + POOL_HIST, :]
        qk_w[0:CONV_HIST, :] = qk_r[T:T + CONV_HIST, :]

        for rb in range(T // OUT_ROWS):
            emit(norm_fill, len(norm_fill) // (T // OUT_ROWS))
            out_block(rb)
        emit(norm_fill, len(norm_fill))

    return block_kernel


def _const_spec(shape):
    return pl.BlockSpec(shape, lambda s: (0,) * len(shape), pipeline_mode=pl.Buffered(1))


@jax.jit
def kernel(x, c, norm_g, w_ada, b_ada, w_in, b_gates, conv_w, conv_b, w_pool, ls_pool,
           mh_norm_g, w_out, final_g):
    B, S, D = x.shape
    T = SEQ_TILE
    assert D == D_MODEL and S % T == 0
    assert w_in.shape == (1, D_MODEL, OFF_G + 2 * N_HEADS)
    tiles_per_seq = S // T
    n_tiles = B * tiles_per_seq

    n_ada = 3
    mod = pl.pallas_call(
        _ada_kernel,
        grid=(n_ada,),
        in_specs=[pl.BlockSpec((B, D), lambda n: (0, 0)),
                  pl.BlockSpec((D, D), lambda n: (0, n)),
                  pl.BlockSpec((1, D), lambda n: (0, n))],
        out_specs=pl.BlockSpec((B, D), lambda n: (0, n)),
        out_shape=jax.ShapeDtypeStruct((B, n_ada * D), F32),
    )(c, w_ada[0], b_ada)
    mod = mod.reshape(B, n_ada, D)

    pad = GATE_PAD - 2 * N_HEADS
    w_in_p = jnp.pad(w_in[0], ((0, 0), (0, pad))).astype(BF16)
    b_gates_p = jnp.pad(b_gates, ((0, 0), (0, pad)))
    n_pairs = len(POOL_WINDOWS) // 2
    wp = w_pool[0].astype(BF16).reshape(n_pairs, 2, POOL_GROUP_DIM, POOL_GROUP_DIM)
    zeros = jnp.zeros((n_pairs, POOL_GROUP_DIM, POOL_GROUP_DIM), BF16)
    wp_bd = jnp.concatenate([jnp.concatenate([wp[:, 0], zeros], axis=2),
                             jnp.concatenate([zeros, wp[:, 1]], axis=2)], axis=1)
    x_tiles = x.reshape(n_tiles, T, D)

    def tile_n(s):
        return (jnp.minimum(s, n_tiles - 1), 0, 0)

    def tile_c(s):
        return (jnp.clip(s - (PIPE_DEPTH - 1), 0, n_tiles - 1), 0, 0)

    out = pl.pallas_call(
        _make_block_kernel(tiles_per_seq, n_tiles),
        grid=(n_tiles + PIPE_DEPTH - 1,),
        in_specs=[
            pl.BlockSpec((None, T, D), tile_n),
            pl.BlockSpec((None, T, D), tile_c),
            _const_spec((B, n_ada, D)),
            _const_spec((1, D)),
            _const_spec((D, D_PROJ_PAD)),
            _const_spec((1, GATE_PAD)),
            _const_spec((CONV_WIDTH, 2 * D_MLSTM)),
            _const_spec((1, 2 * D_MLSTM)),
            _const_spec((n_pairs, MXU_COLS, MXU_COLS)),
            _const_spec((1, D_POOL)),
            _const_spec((1, D_MLSTM)),
            _const_spec((D, D)),
            _const_spec((1, D)),
        ],
        out_specs=pl.BlockSpec((None, T, D), tile_c),
        out_shape=jax.ShapeDtypeStruct((n_tiles, T, D), x.dtype),
        scratch_shapes=[
            *2 * [pltpu.VMEM((T, D), BF16)],
            *2 * [pltpu.VMEM((POOL_HIST + T, D_POOL), BF16)],
            *2 * [pltpu.VMEM((CONV_HIST + T, 2 * D_MLSTM), F32)],
            *2 * [pltpu.VMEM((T, R_COLS), F32)],
            *2 * [pltpu.VMEM((N_CHUNKS, CHUNK, CHUNK), F32)],
            *2 * [pltpu.VMEM((N_CHUNKS, G_ROWS, CHUNK), F32)],
            pltpu.VMEM((len(POOL_WINDOWS), CHUNK, 2 * CHUNK), BF16),
            pltpu.VMEM((T, D), BF16),
            pltpu.VMEM((N_HEADS, HEAD_DIM, 2 * HEAD_DIM), F32),
            pltpu.VMEM((SUBLANES, CHUNK), F32),
            pltpu.VMEM((N_CHUNKS, CHUNK, D_POOL), BF16),
            pltpu.VMEM((N_HEAD_UNITS, CHUNK, HEAD_DIM), F32),
            pltpu.VMEM((N_HEAD_UNITS, HEAD_DIM, CHUNK), BF16),
            pltpu.VMEM((N_HEAD_UNITS, HEAD_DIM, CHUNK), BF16),
            pltpu.VMEM((N_HEAD_UNITS, CHUNK, HEAD_DIM), BF16),
            pltpu.VMEM((N_HEAD_UNITS, CHUNK, CHUNK + HEAD_DIM), BF16),
            pltpu.VMEM((N_HEAD_UNITS, CHUNK + HEAD_DIM, 2 * HEAD_DIM), BF16),
        ],
        compiler_params=pltpu.CompilerParams(
            dimension_semantics=("arbitrary",),
            vmem_limit_bytes=VMEM_LIMIT),
    )(x_tiles, x_tiles, mod, norm_g, w_in_p, b_gates_p, conv_w[0], conv_b,
      wp_bd, ls_pool, mh_norm_g, w_out[0].astype(BF16), final_g.reshape(1, D))
    return out.reshape(B, S, D)
```

```python
import jax
import jax.numpy as jnp
from jax import lax
from jax.experimental import pallas as pl
from jax.experimental.pallas import tpu as pltpu

F32 = jnp.float32
BF16 = jnp.bfloat16

D_MODEL = 1024
D_POOL = 512
D_MLSTM = 512
POOL_WINDOWS = (2, 4, 8, 16)
POOL_GROUP_DIM = 128
N_HEADS = 4
HEAD_DIM = 128
CONV_WIDTH = 4
CHUNK = 128
EPS = 1e-6

LANES = 128
SUBLANES = 8
MXU_COLS = 256
POOL_HIST = CHUNK
POOL_FIX_ROWS = 16
CONV_HIST = SUBLANES
GATE_PAD = LANES

OFF_U = 0
OFF_ZP = 512
OFF_QK = 1024
OFF_V = 2048
OFF_G = 3584
D_PROJ_PAD = OFF_G + GATE_PAD

R_ZP = 0
R_V = 512
R_O = 1024
R_ZM = 1536
R_COLS = 2048

G_A = 0
G_W = SUBLANES
G_DECAY = 2 * SUBLANES
G_ROWS = 3 * SUBLANES

GC_M = 0
GC_INTER = SUBLANES
GC_EFLOOR = 2 * SUBLANES

SEQ_TILE = 512
N_CHUNKS = SEQ_TILE // CHUNK
STRIP = 32
NORM_STRIP = 16
NORM_ROWS = 64
OUT_ROWS = 512
PROJ_ROWS = 512
PROJ_COLS = 2 * MXU_COLS
PIPE_DEPTH = 3
N_PAIRED = 6
N_HEAD_UNITS = N_CHUNKS * N_HEADS
IN_FLIGHT = 3
ROUNDS_PER_TILE = 44
VMEM_LIMIT = 56 * 1024 * 1024


def _silu(z):
    return z * jax.nn.sigmoid(z)


def _log_sigmoid(z):
    return jnp.minimum(z, 0.0) - jnp.log1p(jnp.exp(-jnp.abs(z)))


def _strips(n, step):
    return [slice(r, r + step) for r in range(0, n, step)]


def _ada_kernel(c_ref, w_ref, b_ref, o_ref):
    c = c_ref[...]
    o_ref[...] = jnp.dot(_silu(c), w_ref[...], preferred_element_type=F32,
                         precision=lax.Precision.HIGHEST) + b_ref[...]


def _proj_pieces():
    pieces = []
    for lo in range(OFF_U, OFF_ZP, PROJ_COLS):
        pieces.append((lo, PROJ_COLS, "u", lo - OFF_U))
    for lo in range(OFF_ZP, OFF_QK, PROJ_COLS):
        pieces.append((lo, PROJ_COLS, "rest", R_ZP + lo - OFF_ZP))
    for lo in range(OFF_QK, OFF_V, PROJ_COLS):
        pieces.append((lo, PROJ_COLS, "qk", lo - OFF_QK))
    for lo in range(OFF_V, OFF_G, PROJ_COLS):
        pieces.append((lo, PROJ_COLS, "rest", R_V + lo - OFF_V))
    return pieces


def _make_block_kernel(tiles_per_seq, n_tiles):
    T = SEQ_TILE

    def block_kernel(x_ref, xres_ref, mod_ref, ng_ref, win_ref, bg_ref, cw_ref, cb_ref, wp_ref,
                     ls_ref, mg_ref, wout_ref, fg_ref, o_ref, *scratch):
        s = pl.program_id(0)
        pairs, shared = scratch[:2 * N_PAIRED], scratch[2 * N_PAIRED:]
        band_ref, caug_ref, m_ref, rhs2_ref = shared[0], shared[2], shared[3], shared[-1]

        @pl.when(s == 0)
        def _():
            for ref in pairs + (caug_ref, m_ref):
                ref[...] = jnp.zeros(ref.shape, ref.dtype)
            rhs2_ref[:, 0:CHUNK, HEAD_DIM:] = jnp.ones((N_HEAD_UNITS, CHUNK, HEAD_DIM), BF16)
            t_i = lax.broadcasted_iota(jnp.int32, (CHUNK, 2 * CHUNK), 0)
            r_i = lax.broadcasted_iota(jnp.int32, (CHUNK, 2 * CHUNK), 1) - POOL_HIST
            for g, w in enumerate(POOL_WINDOWS):
                in_window = (r_i <= t_i) & (r_i > t_i - w)
                band = jnp.where(in_window, 1.0 / w, 0.0) - jnp.where(r_i == t_i, 1.0, 0.0)
                band_ref[g] = band.astype(BF16)

        inputs = (x_ref, xres_ref, mod_ref, ng_ref, win_ref, bg_ref, cw_ref, cb_ref, wp_ref,
                  ls_ref, mg_ref, wout_ref, fg_ref, o_ref)
        for parity in range(2):
            written = pairs[parity::2]
            read = pairs[1 - parity::2]
            bufs = tuple(r for wr in zip(written, read) for r in wr)

            @pl.when(s % 2 == parity)
            def _(bufs=bufs):
                step(s, *inputs, *bufs, *shared)

    def step(s, x_ref, xres_ref, mod_ref, ng_ref, win_ref, bg_ref, cw_ref, cb_ref, wp_ref,
             ls_ref, mg_ref, wout_ref, fg_ref, o_ref,
             h_w, h_r, u_w, u_r, qk_w, qk_r, rest_w, rest_r, gcol_w, gcol_r, grow_w, grow_r,
             band_ref, mix_ref, caug_ref, m_ref, pooled_ref, kf_ref, ktbf_ref, kwt_ref, qbf_ref,
             lhs2_ref, rhs2_ref):
        tile_n = jnp.minimum(s, n_tiles - 1)
        tile_c = jnp.clip(s - (PIPE_DEPTH - 1), 0, n_tiles - 1)

        first_of_seq = (s - (PIPE_DEPTH - 1)) % tiles_per_seq == 0
        first_of_seq_g = (s - 1) % tiles_per_seq == 0

        @pl.when(first_of_seq)
        def _():
            u_r[0:POOL_HIST, :] = jnp.zeros((POOL_HIST, D_POOL), BF16)
            qk_r[0:CONV_HIST, :] = jnp.zeros((CONV_HIST, 2 * D_MLSTM), F32)
            caug_ref[...] = jnp.zeros(caug_ref.shape, F32)

        mod_n = mod_ref[tile_n // tiles_per_seq]
        shift = mod_n[0:1, :]
        gscale = ng_ref[...] * (1.0 + mod_n[1:2, :])
        gate = mod_ref[tile_c // tiles_per_seq][2:3, :]

        def norm_block(i):
            for rs in _strips(NORM_ROWS, NORM_STRIP):
                rows = slice(i * NORM_ROWS + rs.start, i * NORM_ROWS + rs.stop)
                x = x_ref[rows, :]
                ms = jnp.mean(x * x, axis=-1, keepdims=True)
                h_w[rows, :] = (x * lax.rsqrt(ms + EPS) * gscale + shift).astype(BF16)

        def proj_piece(piece, rb):
            lo, width, dst, col = piece
            r0, r1 = rb * PROJ_ROWS, (rb + 1) * PROJ_ROWS
            y = jnp.dot(h_r[r0:r1, :], win_ref[:, lo:lo + width],
                        preferred_element_type=F32)
            if dst == "u":
                u_w[POOL_HIST + r0:POOL_HIST + r1, col:col + width] = y.astype(BF16)
            elif dst == "qk":
                qk_w[CONV_HIST + r0:CONV_HIST + r1, col:col + width] = y
            else:
                rest_w[r0:r1, col:col + width] = y

        zero_rows = jnp.zeros((CHUNK - 3 * SUBLANES, CHUNK), F32)

        def gates_tile():
            gts = []
            for rb in range(T // PROJ_ROWS):
                g_pre = jnp.dot(h_r[rb * PROJ_ROWS:(rb + 1) * PROJ_ROWS, :],
                                win_ref[:, OFF_G:D_PROJ_PAD],
                                preferred_element_type=F32) + bg_ref[...]
                for cc in range(PROJ_ROWS // CHUNK):
                    gts.append(g_pre[cc * CHUNK:(cc + 1) * CHUNK, :].T[0:SUBLANES, :])
            yield
            lane = lax.broadcasted_iota(jnp.int32, (SUBLANES, CHUNK), 1)
            bs = [pltpu.roll(_log_sigmoid(gt), N_HEADS, axis=0) for gt in gts]
            k = 1
            while k < CHUNK:
                bs = [x + jnp.where(lane >= k, pltpu.roll(x, k, axis=1), 0.0) for x in bs]
                k *= 2
                yield
            a_s = [gt - b for gt, b in zip(gts, bs)]
            cm = a_s
            k = 1
            while k < CHUNK:
                cm = [jnp.maximum(x, jnp.where(lane >= k, pltpu.roll(x, k, axis=1), -jnp.inf))
                      for x in cm]
                k *= 2
                yield
            m_run = jnp.where(first_of_seq_g, 0.0, m_ref[...])
            for c in range(N_CHUNKS):
                a, b = a_s[c], bs[c]
                big_m = jnp.maximum(m_run, cm[c])
                m_last = jnp.broadcast_to(big_m[:, CHUNK - 1:CHUNK], big_m.shape)
                b_last = jnp.broadcast_to(b[:, CHUNK - 1:CHUNK], b.shape)
                yield
                inter = jnp.exp(m_run - big_m)
                efloor = jnp.exp(-(b + big_m))
                grow_w[c, G_A:G_A + SUBLANES, :] = a
                grow_w[c, G_W:G_W + SUBLANES, :] = jnp.exp(a - m_last)
                grow_w[c, G_DECAY:G_DECAY + SUBLANES, :] = jnp.exp(m_run - m_last)
                gcol_w[c] = jnp.concatenate([big_m, inter, efloor, zero_rows], axis=0).T
                m_run = b_last + m_last
            m_ref[...] = m_run

        def pool_chunk(c, slot):
            r0 = c * CHUNK
            for g, w in enumerate(POOL_WINDOWS):
                cs = slice(g * POOL_GROUP_DIM, (g + 1) * POOL_GROUP_DIM)
                pg = jnp.dot(band_ref[g], u_r[r0:r0 + 2 * CHUNK, cs],
                             preferred_element_type=F32)
                if c == 0:
                    cur = u_r[POOL_HIST:POOL_HIST + POOL_FIX_ROWS, cs].astype(F32)
                    pos = lax.broadcasted_iota(jnp.int32, (POOL_FIX_ROWS, POOL_GROUP_DIM), 0) + 1
                    short = float(w) / jnp.minimum(pos, w).astype(F32) - 1.0
                    scale = 1.0 + first_of_seq.astype(F32) * short
                    fixed = (pg[0:POOL_FIX_ROWS] + cur) * scale - cur
                    pooled_ref[slot, 0:POOL_FIX_ROWS, cs] = fixed.astype(BF16)
                    pooled_ref[slot, POOL_FIX_ROWS:, cs] = pg[POOL_FIX_ROWS:].astype(BF16)
                else:
                    pooled_ref[slot, :, cs] = pg.astype(BF16)
            yield
            for pair in range(len(POOL_WINDOWS) // 2):
                cs = slice(pair * MXU_COLS, (pair + 1) * MXU_COLS)
                y = jnp.dot(pooled_ref[slot, :, cs], wp_ref[pair], preferred_element_type=F32)
                for rs in _strips(CHUNK, STRIP):
                    rows = slice(r0 + rs.start, r0 + rs.stop)
                    zp = rest_r[rows, R_ZP + pair * MXU_COLS:R_ZP + (pair + 1) * MXU_COLS]
                    mix_ref[rows, cs] = (y[rs] * ls_ref[:, cs] * _silu(zp)).astype(BF16)

        def head_chunk(c, hd, slot):
            r0 = c * CHUNK
            hs = slice(hd * HEAD_DIM, (hd + 1) * HEAD_DIM)
            ks = slice(D_MLSTM + hd * HEAD_DIM, D_MLSTM + (hd + 1) * HEAD_DIM)

            def conv(cols_, rs):
                y = cb_ref[:, cols_]
                for t in range(CONV_WIDTH):
                    lo = CONV_HIST + r0 + rs.start - (CONV_WIDTH - 1) + t
                    y = y + cw_ref[t:t + 1, cols_] * qk_r[lo:lo + STRIP, cols_]
                return _silu(y)

            def gcol(col, rs):
                return gcol_r[c, rs, col + hd:col + hd + 1]

            for rs in _strips(CHUNK, STRIP):
                kf_ref[slot, rs, :] = conv(ks, rs) * (HEAD_DIM ** -0.5)
            yield
            kt = kf_ref[slot].T
            w_row = grow_r[c, G_W + hd:G_W + hd + 1, :]
            for rs in _strips(HEAD_DIM, STRIP):
                ktbf_ref[slot, rs, :] = kt[rs].astype(BF16)
                kwt_ref[slot, rs, :] = (kt[rs] * w_row).astype(BF16)
                rhs2_ref[slot, CHUNK + rs.start:CHUNK + rs.stop, :] = caug_ref[hd, rs, :].astype(BF16)
            yield
            for rs in _strips(CHUNK, STRIP):
                rows = slice(r0 + rs.start, r0 + rs.stop)
                q = conv(hs, rs)
                qbf_ref[slot, rs, :] = q.astype(BF16)
                lhs2_ref[slot, rs, CHUNK:] = (q * gcol(GC_INTER, rs)).astype(BF16)
                v = rest_r[rows, R_V + hd * HEAD_DIM:R_V + (hd + 1) * HEAD_DIM]
                rhs2_ref[slot, rs, 0:HEAD_DIM] = v.astype(BF16)
            yield
            sc = jnp.dot(qbf_ref[slot], ktbf_ref[slot], preferred_element_type=F32)
            upd = jnp.dot(kwt_ref[slot], rhs2_ref[slot, 0:CHUNK, :], preferred_element_type=F32)
            dec = grow_r[c, G_DECAY + hd:G_DECAY + hd + 1, :]
            dec2 = jnp.concatenate([dec, dec], axis=1)
            for rs in _strips(HEAD_DIM, STRIP):
                caug_ref[hd, rs, :] = dec2 * caug_ref[hd, rs, :] + upd[rs]
            yield
            a_row = grow_r[c, G_A + hd:G_A + hd + 1, :]
            for rs in _strips(CHUNK, STRIP):
                row_i = lax.broadcasted_iota(jnp.int32, (STRIP, CHUNK), 0) + rs.start
                col_i = lax.broadcasted_iota(jnp.int32, (STRIP, CHUNK), 1)
                dmat = jnp.where(col_i <= row_i, jnp.exp(a_row - gcol(GC_M, rs)), 0.0)
                lhs2_ref[slot, rs, 0:CHUNK] = (sc[rs] * dmat).astype(BF16)
            yield
            numden = jnp.dot(lhs2_ref[slot], rhs2_ref[slot], preferred_element_type=F32)
            yield
            for rs in _strips(CHUNK, STRIP):
                rows = slice(r0 + rs.start, r0 + rs.stop)
                num = numden[rs, :HEAD_DIM]
                den = numden[rs, HEAD_DIM:]
                hm = num / jnp.maximum(jnp.abs(den), gcol(GC_EFLOOR, rs))
                ms = jnp.mean(hm * hm, axis=-1, keepdims=True)
                og = jax.nn.sigmoid(rest_r[rows, R_O + hd * HEAD_DIM:R_O + (hd + 1) * HEAD_DIM])
                zm = rest_r[rows, R_ZM + hd * HEAD_DIM:R_ZM + (hd + 1) * HEAD_DIM]
                hm = hm * lax.rsqrt(ms + EPS) * mg_ref[:, hs]
                mix_ref[rows, D_POOL + hd * HEAD_DIM:D_POOL + (hd + 1) * HEAD_DIM] = (
                    hm * og * _silu(zm)).astype(BF16)

        def out_block(rb):
            rows = slice(rb * OUT_ROWS, (rb + 1) * OUT_ROWS)
            y = jnp.dot(mix_ref[rows, :], wout_ref[...], preferred_element_type=F32)
            o_ref[rows, :] = xres_ref[rows, :] + gate * y
            for rs in _strips(OUT_ROWS, NORM_STRIP):
                rows = slice(rb * OUT_ROWS + rs.start, rb * OUT_ROWS + rs.stop)
                z = o_ref[rows, :]
                ms = jnp.mean(z * z, axis=-1, keepdims=True)
                o_ref[rows, :] = z * lax.rsqrt(ms + EPS) * fg_ref[...]

        fill = [lambda p=p, rb=rb: proj_piece(p, rb)
                for p in _proj_pieces() for rb in range(T // PROJ_ROWS)]
        norm_fill = [lambda i=i: norm_block(i) for i in range(T // NORM_ROWS)]

        def emit(work, n):
            for _ in range(n):
                if work:
                    work.pop(0)()

        def advance(gen):
            try:
                next(gen)
                return True
            except StopIteration:
                return False

        units = []
        for c in range(N_CHUNKS):
            units.append(pool_chunk(c, c))
            units.extend(head_chunk(c, hd, c * N_HEADS + hd) for hd in range(N_HEADS))
        n_rounds = ROUNDS_PER_TILE
        len_fill = len(fill)
        gates = gates_tile()
        advance(gates)
        active = []
        rounds = 0
        while units or active:
            while units and len(active) < IN_FLIGHT:
                active.append(units.pop(0))
            active = [gen for gen in active if advance(gen)]
            rounds += 1
            emit(fill, (rounds * len_fill) // n_rounds - ((rounds - 1) * len_fill) // n_rounds)
            advance(gates)
        for _ in gates:
            pass
        emit(fill, len(fill))

        u_w[0:POOL_HIST, :] = u_r[T:T + POOL_HIST, :]
        qk_w[0:CONV_HIST, :] = qk_r[T:T + CONV_HIST, :]

        for rb in range(T // OUT_ROWS):
            emit(norm_fill, len(norm_fill) // (T // OUT_ROWS))
            out_block(rb)
        emit(norm_fill, len(norm_fill))

    return block_kernel


def _const_spec(shape):
    return pl.BlockSpec(shape, lambda s: (0,) * len(shape), pipeline_mode=pl.Buffered(1))


@jax.jit
def kernel(x, c, norm_g, w_ada, b_ada, w_in, b_gates, conv_w, conv_b, w_pool, ls_pool,
           mh_norm_g, w_out, final_g):
    B, S, D = x.shape
    T = SEQ_TILE
    assert D == D_MODEL and S % T == 0
    assert w_in.shape == (1, D_MODEL, OFF_G + 2 * N_HEADS)
    tiles_per_seq = S // T
    n_tiles = B * tiles_per_seq

    n_ada = 3
    mod = pl.pallas_call(
        _ada_kernel,
        grid=(n_ada,),
        in_specs=[pl.BlockSpec((B, D), lambda n: (0, 0)),
                  pl.BlockSpec((D, D), lambda n: (0, n)),
                  pl.BlockSpec((1, D), lambda n: (0, n))],
        out_specs=pl.BlockSpec((B, D), lambda n: (0, n)),
        out_shape=jax.ShapeDtypeStruct((B, n_ada * D), F32),
    )(c, w_ada[0], b_ada)
    mod = mod.reshape(B, n_ada, D)

    pad = GATE_PAD - 2 * N_HEADS
    w_in_p = jnp.pad(w_in[0], ((0, 0), (0, pad))).astype(BF16)
    b_gates_p = jnp.pad(b_gates, ((0, 0), (0, pad)))
    n_pairs = len(POOL_WINDOWS) // 2
    wp = w_pool[0].astype(BF16).reshape(n_pairs, 2, POOL_GROUP_DIM, POOL_GROUP_DIM)
    zeros = jnp.zeros((n_pairs, POOL_GROUP_DIM, POOL_GROUP_DIM), BF16)
    wp_bd = jnp.concatenate([jnp.concatenate([wp[:, 0], zeros], axis=2),
                             jnp.concatenate([zeros, wp[:, 1]], axis=2)], axis=1)
    x_tiles = x.reshape(n_tiles, T, D)

    def tile_n(s):
        return (jnp.minimum(s, n_tiles - 1), 0, 0)

    def tile_c(s):
        return (jnp.clip(s - (PIPE_DEPTH - 1), 0, n_tiles - 1), 0, 0)

    out = pl.pallas_call(
        _make_block_kernel(tiles_per_seq, n_tiles),
        grid=(n_tiles + PIPE_DEPTH - 1,),
        in_specs=[
            pl.BlockSpec((None, T, D), tile_n),
            pl.BlockSpec((None, T, D), tile_c),
            _const_spec((B, n_ada, D)),
            _const_spec((1, D)),
            _const_spec((D, D_PROJ_PAD)),
            _const_spec((1, GATE_PAD)),
            _const_spec((CONV_WIDTH, 2 * D_MLSTM)),
            _const_spec((1, 2 * D_MLSTM)),
            _const_spec((n_pairs, MXU_COLS, MXU_COLS)),
            _const_spec((1, D_POOL)),
            _const_spec((1, D_MLSTM)),
            _const_spec((D, D)),
            _const_spec((1, D)),
        ],
        out_specs=pl.BlockSpec((None, T, D), tile_c),
        out_shape=jax.ShapeDtypeStruct((n_tiles, T, D), x.dtype),
        scratch_shapes=[
            *2 * [pltpu.VMEM((T, D), BF16)],
            *2 * [pltpu.VMEM((POOL_HIST + T, D_POOL), BF16)],
            *2 * [pltpu.VMEM((CONV_HIST + T, 2 * D_MLSTM), F32)],
            *2 * [pltpu.VMEM((T, R_COLS), F32)],
            *2 * [pltpu.VMEM((N_CHUNKS, CHUNK, CHUNK), F32)],
            *2 * [pltpu.VMEM((N_CHUNKS, G_ROWS, CHUNK), F32)],
            pltpu.VMEM((len(POOL_WINDOWS), CHUNK, 2 * CHUNK), BF16),
            pltpu.VMEM((T, D), BF16),
            pltpu.VMEM((N_HEADS, HEAD_DIM, 2 * HEAD_DIM), F32),
            pltpu.VMEM((SUBLANES, CHUNK), F32),
            pltpu.VMEM((N_CHUNKS, CHUNK, D_POOL), BF16),
            pltpu.VMEM((N_HEAD_UNITS, CHUNK, HEAD_DIM), F32),
            pltpu.VMEM((N_HEAD_UNITS, HEAD_DIM, CHUNK), BF16),
            pltpu.VMEM((N_HEAD_UNITS, HEAD_DIM, CHUNK), BF16),
            pltpu.VMEM((N_HEAD_UNITS, CHUNK, HEAD_DIM), BF16),
            pltpu.VMEM((N_HEAD_UNITS, CHUNK, CHUNK + HEAD_DIM), BF16),
            pltpu.VMEM((N_HEAD_UNITS, CHUNK + HEAD_DIM, 2 * HEAD_DIM), BF16),
        ],
        compiler_params=pltpu.CompilerParams(
            dimension_semantics=("arbitrary",),
            vmem_limit_bytes=VMEM_LIMIT),
    )(x_tiles, x_tiles, mod, norm_g, w_in_p, b_gates_p, conv_w[0], conv_b,
      wp_bd, ls_pool, mh_norm_g, w_out[0].astype(BF16), final_g.reshape(1, D))
    return out.reshape(B, S, D)
```

```python
import jax
import jax.numpy as jnp
from jax import lax
from jax.experimental import pallas as pl
from jax.experimental.pallas import tpu as pltpu

F32 = jnp.float32
BF16 = jnp.bfloat16

D_MODEL = 1024
D_POOL = 512
D_MLSTM = 512
POOL_WINDOWS = (2, 4, 8, 16)
POOL_GROUP_DIM = 128
N_HEADS = 4
HEAD_DIM = 128
CONV_WIDTH = 4
CHUNK = 128
EPS = 1e-6

LANES = 128
SUBLANES = 8
MXU_COLS = 256
POOL_HIST = CHUNK
POOL_FIX_ROWS = 16
CONV_HIST = SUBLANES
GATE_PAD = LANES

OFF_U = 0
OFF_ZP = 512
OFF_QK = 1024
OFF_V = 2048
OFF_G = 3584
D_PROJ_PAD = OFF_G + GATE_PAD

R_ZP = 0
R_V = 512
R_O = 1024
R_ZM = 1536
R_COLS = 2048

G_A = 0
G_W = SUBLANES
G_DECAY = 2 * SUBLANES
G_ROWS = 3 * SUBLANES

GC_M = 0
GC_INTER = SUBLANES
GC_EFLOOR = 2 * SUBLANES

SEQ_TILE = 512
N_CHUNKS = SEQ_TILE // CHUNK
STRIP = 128
NORM_STRIP = 16
NORM_ROWS = 64
OUT_ROWS = 256
PROJ_ROWS = 256
PROJ_COLS = 2 * MXU_COLS
PIPE_DEPTH = 3
N_PAIRED = 6
N_HEAD_UNITS = N_CHUNKS * N_HEADS
IN_FLIGHT = 3
ROUNDS_PER_TILE = 44
POOL_ORDER = (0, 1, 0, 2, 1, 3, 2, 3)
VMEM_LIMIT = 56 * 1024 * 1024


def _silu(z):
    return z * jax.nn.sigmoid(z)


def _log_sigmoid(z):
    return jnp.minimum(z, 0.0) - jnp.log1p(jnp.exp(-jnp.abs(z)))


def _strips(n, step):
    return [slice(r, r + step) for r in range(0, n, step)]


def _merge_evenly(lists):
    merged = []
    total = sum(len(l) for l in lists)
    taken = [0] * len(lists)
    for i in range(1, total + 1):
        for j, l in enumerate(lists):
            while taken[j] < (i * len(l)) // total:
                merged.append(l[taken[j]])
                taken[j] += 1
    return merged


def _ada_kernel(c_ref, w_ref, b_ref, o_ref):
    c = c_ref[...]
    o_ref[...] = jnp.dot(_silu(c), w_ref[...], preferred_element_type=F32,
                         precision=lax.Precision.HIGHEST) + b_ref[...]


def _proj_pieces():
    pieces = []
    for lo in range(OFF_U, OFF_ZP, PROJ_COLS):
        pieces.append((lo, PROJ_COLS, "u", lo - OFF_U))
    for lo in range(OFF_ZP, OFF_QK, PROJ_COLS):
        pieces.append((lo, PROJ_COLS, "rest", R_ZP + lo - OFF_ZP))
    for lo in range(OFF_QK, OFF_V, PROJ_COLS):
        pieces.append((lo, PROJ_COLS, "qk", lo - OFF_QK))
    for lo in range(OFF_V, OFF_G, PROJ_COLS):
        pieces.append((lo, PROJ_COLS, "rest", R_V + lo - OFF_V))
    return pieces


def _make_block_kernel(tiles_per_seq, n_tiles):
    T = SEQ_TILE

    def block_kernel(x_ref, xres_ref, mod_ref, ng_ref, win_ref, bg_ref, cw_ref, cb_ref, wp_ref,
                     ls_ref, mg_ref, wout_ref, fg_ref, o_ref, *scratch):
        s = pl.program_id(0)
        pairs, shared = scratch[:2 * N_PAIRED], scratch[2 * N_PAIRED:]
        band_ref, caug_ref, m_ref, rhs2_ref = shared[0], shared[2], shared[3], shared[-1]

        @pl.when(s == 0)
        def _():
            for ref in pairs + (caug_ref, m_ref):
                ref[...] = jnp.zeros(ref.shape, ref.dtype)
            rhs2_ref[:, 0:CHUNK, HEAD_DIM:] = jnp.ones((N_HEAD_UNITS, CHUNK, HEAD_DIM), BF16)
            t_i = lax.broadcasted_iota(jnp.int32, (CHUNK, 2 * CHUNK), 0)
            r_i = lax.broadcasted_iota(jnp.int32, (CHUNK, 2 * CHUNK), 1) - POOL_HIST
            for g, w in enumerate(POOL_WINDOWS):
                in_window = (r_i <= t_i) & (r_i > t_i - w)
                band = jnp.where(in_window, 1.0 / w, 0.0) - jnp.where(r_i == t_i, 1.0, 0.0)
                band_ref[g] = band.astype(BF16)

        inputs = (x_ref, xres_ref, mod_ref, ng_ref, win_ref, bg_ref, cw_ref, cb_ref, wp_ref,
                  ls_ref, mg_ref, wout_ref, fg_ref, o_ref)
        for parity in range(2):
            written = pairs[parity::2]
            read = pairs[1 - parity::2]
            bufs = tuple(r for wr in zip(written, read) for r in wr)

            @pl.when(s % 2 == parity)
            def _(bufs=bufs):
                step(s, *inputs, *bufs, *shared)

    def step(s, x_ref, xres_ref, mod_ref, ng_ref, win_ref, bg_ref, cw_ref, cb_ref, wp_ref,
             ls_ref, mg_ref, wout_ref, fg_ref, o_ref,
             h_w, h_r, u_w, u_r, qk_w, qk_r, rest_w, rest_r, gcol_w, gcol_r, grow_w, grow_r,
             band_ref, mix_ref, caug_ref, m_ref, pooled_ref, kf_ref, ktbf_ref, kwt_ref, qbf_ref,
             lhs2_ref, rhs2_ref):
        tile_n = jnp.minimum(s, n_tiles - 1)
        tile_c = jnp.clip(s - (PIPE_DEPTH - 1), 0, n_tiles - 1)

        first_of_seq = (s - (PIPE_DEPTH - 1)) % tiles_per_seq == 0
        first_of_seq_g = (s - 1) % tiles_per_seq == 0

        @pl.when(first_of_seq)
        def _():
            u_r[0:POOL_HIST, :] = jnp.zeros((POOL_HIST, D_POOL), BF16)
            qk_r[0:CONV_HIST, :] = jnp.zeros((CONV_HIST, 2 * D_MLSTM), F32)
            caug_ref[...] = jnp.zeros(caug_ref.shape, F32)

        mod_n = mod_ref[tile_n // tiles_per_seq]
        shift = mod_n[0:1, :]
        gscale = ng_ref[...] * (1.0 + mod_n[1:2, :])
        gate = mod_ref[tile_c // tiles_per_seq][2:3, :]

        def norm_block(i):
            for rs in _strips(NORM_ROWS, NORM_STRIP):
                rows = slice(i * NORM_ROWS + rs.start, i * NORM_ROWS + rs.stop)
                x = x_ref[rows, :]
                ms = jnp.mean(x * x, axis=-1, keepdims=True)
                h_w[rows, :] = (x * lax.rsqrt(ms + EPS) * gscale + shift).astype(BF16)

        def proj_piece(piece, rb):
            lo, width, dst, col = piece
            r0, r1 = rb * PROJ_ROWS, (rb + 1) * PROJ_ROWS
            y = jnp.dot(h_r[r0:r1, :], win_ref[:, lo:lo + width],
                        preferred_element_type=F32)
            if dst == "u":
                u_w[POOL_HIST + r0:POOL_HIST + r1, col:col + width] = y.astype(BF16)
            elif dst == "qk":
                qk_w[CONV_HIST + r0:CONV_HIST + r1, col:col + width] = y
            else:
                rest_w[r0:r1, col:col + width] = y

        zero_rows = jnp.zeros((CHUNK - 3 * SUBLANES, CHUNK), F32)

        def gates_tile():
            gts = []
            for rb in range(T // PROJ_ROWS):
                g_pre = jnp.dot(h_r[rb * PROJ_ROWS:(rb + 1) * PROJ_ROWS, :],
                                win_ref[:, OFF_G:D_PROJ_PAD],
                                preferred_element_type=F32) + bg_ref[...]
                for cc in range(PROJ_ROWS // CHUNK):
                    gts.append(g_pre[cc * CHUNK:(cc + 1) * CHUNK, :].T[0:SUBLANES, :])
            yield
            lane = lax.broadcasted_iota(jnp.int32, (SUBLANES, CHUNK), 1)
            bs = [pltpu.roll(_log_sigmoid(gt), N_HEADS, axis=0) for gt in gts]
            k = 1
            while k < CHUNK:
                bs = [x + jnp.where(lane >= k, pltpu.roll(x, k, axis=1), 0.0) for x in bs]
                k *= 2
                yield
            a_s = [gt - b for gt, b in zip(gts, bs)]
            cm = a_s
            k = 1
            while k < CHUNK:
                cm = [jnp.maximum(x, jnp.where(lane >= k, pltpu.roll(x, k, axis=1), -jnp.inf))
                      for x in cm]
                k *= 2
                yield
            m_run = jnp.where(first_of_seq_g, 0.0, m_ref[...])
            for c in range(N_CHUNKS):
                a, b = a_s[c], bs[c]
                big_m = jnp.maximum(m_run, cm[c])
                m_last = jnp.broadcast_to(big_m[:, CHUNK - 1:CHUNK], big_m.shape)
                b_last = jnp.broadcast_to(b[:, CHUNK - 1:CHUNK], b.shape)
                yield
                inter = jnp.exp(m_run - big_m)
                efloor = jnp.exp(-(b + big_m))
                grow_w[c, G_A:G_A + SUBLANES, :] = a
                grow_w[c, G_W:G_W + SUBLANES, :] = jnp.exp(a - m_last)
                grow_w[c, G_DECAY:G_DECAY + SUBLANES, :] = jnp.exp(m_run - m_last)
                gcol_w[c] = jnp.concatenate([big_m, inter, efloor, zero_rows], axis=0).T
                m_run = b_last + m_last
            m_ref[...] = m_run

        def pool_chunk(c, slot):
            r0 = c * CHUNK
            for g, w in enumerate(POOL_WINDOWS):
                cs = slice(g * POOL_GROUP_DIM, (g + 1) * POOL_GROUP_DIM)
                pg = jnp.dot(band_ref[g], u_r[r0:r0 + 2 * CHUNK, cs],
                             preferred_element_type=F32)
                if c == 0:
                    cur = u_r[POOL_HIST:POOL_HIST + POOL_FIX_ROWS, cs].astype(F32)
                    pos = lax.broadcasted_iota(jnp.int32, (POOL_FIX_ROWS, POOL_GROUP_DIM), 0) + 1
                    short = float(w) / jnp.minimum(pos, w).astype(F32) - 1.0
                    scale = 1.0 + first_of_seq.astype(F32) * short
                    fixed = (pg[0:POOL_FIX_ROWS] + cur) * scale - cur
                    pooled_ref[slot, 0:POOL_FIX_ROWS, cs] = fixed.astype(BF16)
                    pooled_ref[slot, POOL_FIX_ROWS:, cs] = pg[POOL_FIX_ROWS:].astype(BF16)
                else:
                    pooled_ref[slot, :, cs] = pg.astype(BF16)
            yield
            for pair in range(len(POOL_WINDOWS) // 2):
                cs = slice(pair * MXU_COLS, (pair + 1) * MXU_COLS)
                y = jnp.dot(pooled_ref[slot, :, cs], wp_ref[pair], preferred_element_type=F32)
                for rs in _strips(CHUNK, STRIP):
                    rows = slice(r0 + rs.start, r0 + rs.stop)
                    zp = rest_r[rows, R_ZP + pair * MXU_COLS:R_ZP + (pair + 1) * MXU_COLS]
                    mix_ref[rows, cs] = (y[rs] * ls_ref[:, cs] * _silu(zp)).astype(BF16)

        def head_chunk(c, hd, slot):
            r0 = c * CHUNK
            hs = slice(hd * HEAD_DIM, (hd + 1) * HEAD_DIM)
            ks = slice(D_MLSTM + hd * HEAD_DIM, D_MLSTM + (hd + 1) * HEAD_DIM)

            def conv(cols_, rs):
                y = cb_ref[:, cols_]
                for t in range(CONV_WIDTH):
                    lo = CONV_HIST + r0 + rs.start - (CONV_WIDTH - 1) + t
                    y = y + cw_ref[t:t + 1, cols_] * qk_r[lo:lo + STRIP, cols_]
                return _silu(y)

            def gcol(col, rs):
                return gcol_r[c, rs, col + hd:col + hd + 1]

            for rs in _strips(CHUNK, STRIP):
                kf_ref[slot, rs, :] = conv(ks, rs) * (HEAD_DIM ** -0.5)
            yield
            kt = kf_ref[slot].T
            w_row = grow_r[c, G_W + hd:G_W + hd + 1, :]
            for rs in _strips(HEAD_DIM, STRIP):
                ktbf_ref[slot, rs, :] = kt[rs].astype(BF16)
                kwt_ref[slot, rs, :] = (kt[rs] * w_row).astype(BF16)
                rhs2_ref[slot, CHUNK + rs.start:CHUNK + rs.stop, :] = caug_ref[hd, rs, :].astype(BF16)
            yield
            for rs in _strips(CHUNK, STRIP):
                rows = slice(r0 + rs.start, r0 + rs.stop)
                q = conv(hs, rs)
                qbf_ref[slot, rs, :] = q.astype(BF16)
                lhs2_ref[slot, rs, CHUNK:] = (q * gcol(GC_INTER, rs)).astype(BF16)
                v = rest_r[rows, R_V + hd * HEAD_DIM:R_V + (hd + 1) * HEAD_DIM]
                rhs2_ref[slot, rs, 0:HEAD_DIM] = v.astype(BF16)
            yield
            sc = jnp.dot(qbf_ref[slot], ktbf_ref[slot], preferred_element_type=F32)
            upd = jnp.dot(kwt_ref[slot], rhs2_ref[slot, 0:CHUNK, :], preferred_element_type=F32)
            dec = grow_r[c, G_DECAY + hd:G_DECAY + hd + 1, :]
            dec2 = jnp.concatenate([dec, dec], axis=1)
            for rs in _strips(HEAD_DIM, STRIP):
                caug_ref[hd, rs, :] = dec2 * caug_ref[hd, rs, :] + upd[rs]
            yield
            a_row = grow_r[c, G_A + hd:G_A + hd + 1, :]
            for rs in _strips(CHUNK, STRIP):
                row_i = lax.broadcasted_iota(jnp.int32, (STRIP, CHUNK), 0) + rs.start
                col_i = lax.broadcasted_iota(jnp.int32, (STRIP, CHUNK), 1)
                dmat = jnp.where(col_i <= row_i, jnp.exp(a_row - gcol(GC_M, rs)), 0.0)
                lhs2_ref[slot, rs, 0:CHUNK] = (sc[rs] * dmat).astype(BF16)
            yield
            numden = jnp.dot(lhs2_ref[slot], rhs2_ref[slot], preferred_element_type=F32)
            yield
            for rs in _strips(CHUNK, STRIP):
                rows = slice(r0 + rs.start, r0 + rs.stop)
                num = numden[rs, :HEAD_DIM]
                den = numden[rs, HEAD_DIM:]
                hm = num / jnp.maximum(jnp.abs(den), gcol(GC_EFLOOR, rs))
                ms = jnp.mean(hm * hm, axis=-1, keepdims=True)
                og = jax.nn.sigmoid(rest_r[rows, R_O + hd * HEAD_DIM:R_O + (hd + 1) * HEAD_DIM])
                zm = rest_r[rows, R_ZM + hd * HEAD_DIM:R_ZM + (hd + 1) * HEAD_DIM]
                hm = hm * lax.rsqrt(ms + EPS) * mg_ref[:, hs]
                mix_ref[rows, D_POOL + hd * HEAD_DIM:D_POOL + (hd + 1) * HEAD_DIM] = (
                    hm * og * _silu(zm)).astype(BF16)

        def out_block(rb):
            rows = slice(rb * OUT_ROWS, (rb + 1) * OUT_ROWS)
            y = jnp.dot(mix_ref[rows, :], wout_ref[...], preferred_element_type=F32)
            o_ref[rows, :] = xres_ref[rows, :] + gate * y
            for rs in _strips(OUT_ROWS, NORM_STRIP):
                rows = slice(rb * OUT_ROWS + rs.start, rb * OUT_ROWS + rs.stop)
                z = o_ref[rows, :]
                ms = jnp.mean(z * z, axis=-1, keepdims=True)
                o_ref[rows, :] = z * lax.rsqrt(ms + EPS) * fg_ref[...]

        proj_fill = [lambda p=p, rb=rb: proj_piece(p, rb)
                     for p in _proj_pieces() for rb in range(T // PROJ_ROWS)]
        pools = [pool_chunk(c, c) for c in range(N_CHUNKS)]
        pool_fill = [lambda c=c: next(pools[c], None) for c in POOL_ORDER]
        fill = _merge_evenly([proj_fill, pool_fill])
        norm_fill = [lambda i=i: norm_block(i) for i in range(T // NORM_ROWS)]

        def emit(work, n):
            for _ in range(n):
                if work:
                    work.pop(0)()

        def advance(gen):
            try:
                next(gen)
                return True
            except StopIteration:
                return False

        units = [head_chunk(c, hd, c * N_HEADS + hd)
                 for c in range(N_CHUNKS) for hd in range(N_HEADS)]
        n_rounds = ROUNDS_PER_TILE
        len_fill = len(fill)
        gates = gates_tile()
        advance(gates)
        active = []
        rounds = 0
        while units or active:
            while units and len(active) < IN_FLIGHT:
                active.append(units.pop(0))
            active = [gen for gen in active if advance(gen)]
            rounds += 1
            emit(fill, (rounds * len_fill) // n_rounds - ((rounds - 1) * len_fill) // n_rounds)
            advance(gates)
        for _ in gates:
            pass
        emit(fill, len(fill))

        u_w[0:POOL_HIST, :] = u_r[T:T + POOL_HIST, :]
        qk_w[0:CONV_HIST, :] = qk_r[T:T + CONV_HIST, :]

        for rb in range(T // OUT_ROWS):
            emit(norm_fill, len(norm_fill) // (T // OUT_ROWS))
            out_block(rb)
        emit(norm_fill, len(norm_fill))

    return block_kernel


def _const_spec(shape):
    return pl.BlockSpec(shape, lambda s: (0,) * len(shape), pipeline_mode=pl.Buffered(1))


@jax.jit
def kernel(x, c, norm_g, w_ada, b_ada, w_in, b_gates, conv_w, conv_b, w_pool, ls_pool,
           mh_norm_g, w_out, final_g):
    B, S, D = x.shape
    T = SEQ_TILE
    assert D == D_MODEL and S % T == 0
    assert w_in.shape == (1, D_MODEL, OFF_G + 2 * N_HEADS)
    tiles_per_seq = S // T
    n_tiles = B * tiles_per_seq

    n_ada = 3
    mod = pl.pallas_call(
        _ada_kernel,
        grid=(n_ada,),
        in_specs=[pl.BlockSpec((B, D), lambda n: (0, 0)),
                  pl.BlockSpec((D, D), lambda n: (0, n)),
                  pl.BlockSpec((1, D), lambda n: (0, n))],
        out_specs=pl.BlockSpec((B, D), lambda n: (0, n)),
        out_shape=jax.ShapeDtypeStruct((B, n_ada * D), F32),
    )(c, w_ada[0], b_ada)
    mod = mod.reshape(B, n_ada, D)

    pad = GATE_PAD - 2 * N_HEADS
    w_in_p = jnp.pad(w_in[0], ((0, 0), (0, pad))).astype(BF16)
    b_gates_p = jnp.pad(b_gates, ((0, 0), (0, pad)))
    n_pairs = len(POOL_WINDOWS) // 2
    wp = w_pool[0].astype(BF16).reshape(n_pairs, 2, POOL_GROUP_DIM, POOL_GROUP_DIM)
    zeros = jnp.zeros((n_pairs, POOL_GROUP_DIM, POOL_GROUP_DIM), BF16)
    wp_bd = jnp.concatenate([jnp.concatenate([wp[:, 0], zeros], axis=2),
                             jnp.concatenate([zeros, wp[:, 1]], axis=2)], axis=1)
    x_tiles = x.reshape(n_tiles, T, D)

    def tile_n(s):
        return (jnp.minimum(s, n_tiles - 1), 0, 0)

    def tile_c(s):
        return (jnp.clip(s - (PIPE_DEPTH - 1), 0, n_tiles - 1), 0, 0)

    out = pl.pallas_call(
        _make_block_kernel(tiles_per_seq, n_tiles),
        grid=(n_tiles + PIPE_DEPTH - 1,),
        in_specs=[
            pl.BlockSpec((None, T, D), tile_n),
            pl.BlockSpec((None, T, D), tile_c),
            _const_spec((B, n_ada, D)),
            _const_spec((1, D)),
            _const_spec((D, D_PROJ_PAD)),
            _const_spec((1, GATE_PAD)),
            _const_spec((CONV_WIDTH, 2 * D_MLSTM)),
            _const_spec((1, 2 * D_MLSTM)),
            _const_spec((n_pairs, MXU_COLS, MXU_COLS)),
            _const_spec((1, D_POOL)),
            _const_spec((1, D_MLSTM)),
            _const_spec((D, D)),
            _const_spec((1, D)),
        ],
        out_specs=pl.BlockSpec((None, T, D), tile_c),
        out_shape=jax.ShapeDtypeStruct((n_tiles, T, D), x.dtype),
        scratch_shapes=[
            *2 * [pltpu.VMEM((T, D), BF16)],
            *2 * [pltpu.VMEM((POOL_HIST + T, D_POOL), BF16)],
            *2 * [pltpu.VMEM((CONV_HIST + T, 2 * D_MLSTM), F32)],
            *2 * [pltpu.VMEM((T, R_COLS), F32)],
            *2 * [pltpu.VMEM((N_CHUNKS, CHUNK, CHUNK), F32)],
            *2 * [pltpu.VMEM((N_CHUNKS, G_ROWS, CHUNK), F32)],
            pltpu.VMEM((len(POOL_WINDOWS), CHUNK, 2 * CHUNK), BF16),
            pltpu.VMEM((T, D), BF16),
            pltpu.VMEM((N_HEADS, HEAD_DIM, 2 * HEAD_DIM), F32),
            pltpu.VMEM((SUBLANES, CHUNK), F32),
            pltpu.VMEM((N_CHUNKS, CHUNK, D_POOL), BF16),
            pltpu.VMEM((N_HEAD_UNITS, CHUNK, HEAD_DIM), F32),
            pltpu.VMEM((N_HEAD_UNITS, HEAD_DIM, CHUNK), BF16),
            pltpu.VMEM((N_HEAD_UNITS, HEAD_DIM, CHUNK), BF16),
            pltpu.VMEM((N_HEAD_UNITS, CHUNK, HEAD_DIM), BF16),
            pltpu.VMEM((N_HEAD_UNITS, CHUNK, CHUNK + HEAD_DIM), BF16),
            pltpu.VMEM((N_HEAD_UNITS, CHUNK + HEAD_DIM, 2 * HEAD_DIM), BF16),
        ],
        compiler_params=pltpu.CompilerParams(
            dimension_semantics=("arbitrary",),
            vmem_limit_bytes=VMEM_LIMIT),
    )(x_tiles, x_tiles, mod, norm_g, w_in_p, b_gates_p, conv_w[0], conv_b,
      wp_bd, ls_pool, mh_norm_g, w_out[0].astype(BF16), final_g.reshape(1, D))
    return out.reshape(B, S, D)
```

```python
import jax
import jax.numpy as jnp
from jax import lax
from jax.experimental import pallas as pl
from jax.experimental.pallas import tpu as pltpu

F32 = jnp.float32
BF16 = jnp.bfloat16

D_MODEL = 1024
D_POOL = 512
D_MLSTM = 512
POOL_WINDOWS = (2, 4, 8, 16)
POOL_GROUP_DIM = 128
N_HEADS = 4
HEAD_DIM = 128
CONV_WIDTH = 4
CHUNK = 128
EPS = 1e-6

LANES = 128
SUBLANES = 8
MXU_COLS = 256
POOL_HIST = CHUNK
POOL_FIX_ROWS = 16
CONV_HIST = SUBLANES
GATE_PAD = LANES

OFF_U = 0
OFF_ZP = 512
OFF_QK = 1024
OFF_V = 2048
OFF_G = 3584
D_PROJ_PAD = OFF_G + GATE_PAD

R_ZP = 0
R_V = 512
R_O = 1024
R_ZM = 1536
R_COLS = 2048

G_A = 0
G_W = SUBLANES
G_DECAY = 2 * SUBLANES
G_ROWS = 3 * SUBLANES

GC_M = 0
GC_INTER = SUBLANES
GC_EFLOOR = 2 * SUBLANES

SEQ_TILE = 512
N_CHUNKS = SEQ_TILE // CHUNK
STRIP = 128
NORM_STRIP = 16
NORM_ROWS = 64
OUT_ROWS = 256
PROJ_ROWS = 256
PROJ_COLS = 2 * MXU_COLS
PIPE_DEPTH = 3
N_PAIRED = 6
N_HEAD_UNITS = N_CHUNKS * N_HEADS
IN_FLIGHT = 3
ROUNDS_PER_TILE = 44
POOL_ORDER = (0, 1, 0, 2, 1, 3, 2, 3)
VMEM_LIMIT = 56 * 1024 * 1024


def _silu(z):
    return z * jax.nn.sigmoid(z)


def _log_sigmoid(z):
    return jnp.minimum(z, 0.0) - jnp.log1p(jnp.exp(-jnp.abs(z)))


def _strips(n, step):
    return [slice(r, r + step) for r in range(0, n, step)]


def _merge_evenly(lists):
    merged = []
    total = sum(len(l) for l in lists)
    taken = [0] * len(lists)
    for i in range(1, total + 1):
        for j, l in enumerate(lists):
            while taken[j] < (i * len(l)) // total:
                merged.append(l[taken[j]])
                taken[j] += 1
    return merged


def _ada_kernel(c_ref, w_ref, b_ref, o_ref):
    c = c_ref[...]
    o_ref[...] = jnp.dot(_silu(c), w_ref[...], preferred_element_type=F32,
                         precision=lax.Precision.HIGHEST) + b_ref[...]


def _proj_pieces():
    pieces = []
    for lo in range(OFF_U, OFF_ZP, PROJ_COLS):
        pieces.append((lo, PROJ_COLS, "u", lo - OFF_U))
    for lo in range(OFF_ZP, OFF_QK, PROJ_COLS):
        pieces.append((lo, PROJ_COLS, "rest", R_ZP + lo - OFF_ZP))
    for lo in range(OFF_QK, OFF_V, PROJ_COLS):
        pieces.append((lo, PROJ_COLS, "qk", lo - OFF_QK))
    for lo in range(OFF_V, OFF_G, PROJ_COLS):
        pieces.append((lo, PROJ_COLS, "rest", R_V + lo - OFF_V))
    return pieces


def _make_block_kernel(tiles_per_seq, n_tiles):
    T = SEQ_TILE

    def block_kernel(x_ref, xres_ref, mod_ref, ng_ref, win_ref, bg_ref, cw_ref, cb_ref, wp_ref,
                     ls_ref, mg_ref, wout_ref, fg_ref, o_ref, *scratch):
        s = pl.program_id(0)
        pairs, shared = scratch[:2 * N_PAIRED], scratch[2 * N_PAIRED:]
        band_ref, caug_ref, m_ref, rhs2_ref = shared[0], shared[2], shared[3], shared[-1]

        @pl.when(s == 0)
        def _():
            for ref in pairs + (caug_ref, m_ref):
                ref[...] = jnp.zeros(ref.shape, ref.dtype)
            rhs2_ref[:, 0:CHUNK, HEAD_DIM:] = jnp.ones((N_HEAD_UNITS, CHUNK, HEAD_DIM), BF16)
            t_i = lax.broadcasted_iota(jnp.int32, (CHUNK, 2 * CHUNK), 0)
            r_i = lax.broadcasted_iota(jnp.int32, (CHUNK, 2 * CHUNK), 1) - POOL_HIST
            for g, w in enumerate(POOL_WINDOWS):
                in_window = (r_i <= t_i) & (r_i > t_i - w)
                band = jnp.where(in_window, 1.0 / w, 0.0) - jnp.where(r_i == t_i, 1.0, 0.0)
                band_ref[g] = band.astype(BF16)

        inputs = (x_ref, xres_ref, mod_ref, ng_ref, win_ref, bg_ref, cw_ref, cb_ref, wp_ref,
                  ls_ref, mg_ref, wout_ref, fg_ref, o_ref)
        for parity in range(2):
            written = pairs[parity::2]
            read = pairs[1 - parity::2]
            bufs = tuple(r for wr in zip(written, read) for r in wr)

            @pl.when(s % 2 == parity)
            def _(bufs=bufs):
                step(s, *inputs, *bufs, *shared)

    def step(s, x_ref, xres_ref, mod_ref, ng_ref, win_ref, bg_ref, cw_ref, cb_ref, wp_ref,
             ls_ref, mg_ref, wout_ref, fg_ref, o_ref,
             h_w, h_r, u_w, u_r, qk_w, qk_r, rest_w, rest_r, gcol_w, gcol_r, grow_w, grow_r,
             band_ref, mix_ref, caug_ref, m_ref, pooled_ref, kf_ref, ktbf_ref, kwt_ref, qbf_ref,
             lhs2_ref, rhs2_ref):
        tile_n = jnp.minimum(s, n_tiles - 1)
        tile_c = jnp.clip(s - (PIPE_DEPTH - 1), 0, n_tiles - 1)

        first_of_seq = (s - (PIPE_DEPTH - 1)) % tiles_per_seq == 0
        first_of_seq_g = (s - 1) % tiles_per_seq == 0

        @pl.when(first_of_seq)
        def _():
            u_r[0:POOL_HIST, :] = jnp.zeros((POOL_HIST, D_POOL), BF16)
            qk_r[0:CONV_HIST, :] = jnp.zeros((CONV_HIST, 2 * D_MLSTM), F32)
            caug_ref[...] = jnp.zeros(caug_ref.shape, F32)

        mod_n = mod_ref[tile_n // tiles_per_seq]
        shift = mod_n[0:1, :]
        gscale = ng_ref[...] * (1.0 + mod_n[1:2, :])
        gate = mod_ref[tile_c // tiles_per_seq][2:3, :]

        def norm_block(i):
            for rs in _strips(NORM_ROWS, NORM_STRIP):
                rows = slice(i * NORM_ROWS + rs.start, i * NORM_ROWS + rs.stop)
                x = x_ref[rows, :]
                ms = jnp.mean(x * x, axis=-1, keepdims=True)
                h_w[rows, :] = (x * lax.rsqrt(ms + EPS) * gscale + shift).astype(BF16)

        def proj_piece(piece, rb):
            lo, width, dst, col = piece
            r0, r1 = rb * PROJ_ROWS, (rb + 1) * PROJ_ROWS
            y = jnp.dot(h_r[r0:r1, :], win_ref[:, lo:lo + width],
                        preferred_element_type=F32)
            if dst == "u":
                u_w[POOL_HIST + r0:POOL_HIST + r1, col:col + width] = y.astype(BF16)
            elif dst == "qk":
                qk_w[CONV_HIST + r0:CONV_HIST + r1, col:col + width] = y
            else:
                rest_w[r0:r1, col:col + width] = y

        zero_rows = jnp.zeros((CHUNK - 3 * SUBLANES, CHUNK), F32)

        def gates_tile():
            gts = []
            for rb in range(T // PROJ_ROWS):
                g_pre = jnp.dot(h_r[rb * PROJ_ROWS:(rb + 1) * PROJ_ROWS, :],
                                win_ref[:, OFF_G:D_PROJ_PAD],
                                preferred_element_type=F32) + bg_ref[...]
                for cc in range(PROJ_ROWS // CHUNK):
                    gts.append(g_pre[cc * CHUNK:(cc + 1) * CHUNK, :].T[0:SUBLANES, :])
            yield
            lane = lax.broadcasted_iota(jnp.int32, (SUBLANES, CHUNK), 1)
            bs = [pltpu.roll(_log_sigmoid(gt), N_HEADS, axis=0) for gt in gts]
            k = 1
            while k < CHUNK:
                bs = [x + jnp.where(lane >= k, pltpu.roll(x, k, axis=1), 0.0) for x in bs]
                k *= 2
                yield
            a_s = [gt - b for gt, b in zip(gts, bs)]
            cm = a_s
            k = 1
            while k < CHUNK:
                cm = [jnp.maximum(x, jnp.where(lane >= k, pltpu.roll(x, k, axis=1), -jnp.inf))
                      for x in cm]
                k *= 2
                yield
            m_run = jnp.where(first_of_seq_g, 0.0, m_ref[...])
            for c in range(N_CHUNKS):
                a, b = a_s[c], bs[c]
                big_m = jnp.maximum(m_run, cm[c])
                m_last = jnp.broadcast_to(big_m[:, CHUNK - 1:CHUNK], big_m.shape)
                b_last = jnp.broadcast_to(b[:, CHUNK - 1:CHUNK], b.shape)
                yield
                inter = jnp.exp(m_run - big_m)
                efloor = jnp.exp(-(b + big_m))
                grow_w[c, G_A:G_A + SUBLANES, :] = a
                grow_w[c, G_W:G_W + SUBLANES, :] = jnp.exp(a - m_last)
                grow_w[c, G_DECAY:G_DECAY + SUBLANES, :] = jnp.exp(m_run - m_last)
                gcol_w[c] = jnp.concatenate([big_m, inter, efloor, zero_rows], axis=0).T
                m_run = b_last + m_last
            m_ref[...] = m_run

        def pool_chunk(c, slot):
            r0 = c * CHUNK
            for g, w in enumerate(POOL_WINDOWS):
                cs = slice(g * POOL_GROUP_DIM, (g + 1) * POOL_GROUP_DIM)
                pg = jnp.dot(band_ref[g], u_r[r0:r0 + 2 * CHUNK, cs],
                             preferred_element_type=F32)
                if c == 0:
                    cur = u_r[POOL_HIST:POOL_HIST + POOL_FIX_ROWS, cs].astype(F32)
                    pos = lax.broadcasted_iota(jnp.int32, (POOL_FIX_ROWS, POOL_GROUP_DIM), 0) + 1
                    short = float(w) / jnp.minimum(pos, w).astype(F32) - 1.0
                    scale = 1.0 + first_of_seq.astype(F32) * short
                    fixed = (pg[0:POOL_FIX_ROWS] + cur) * scale - cur
                    pooled_ref[slot, 0:POOL_FIX_ROWS, cs] = fixed.astype(BF16)
                    pooled_ref[slot, POOL_FIX_ROWS:, cs] = pg[POOL_FIX_ROWS:].astype(BF16)
                else:
                    pooled_ref[slot, :, cs] = pg.astype(BF16)
            yield
            for pair in range(len(POOL_WINDOWS) // 2):
                cs = slice(pair * MXU_COLS, (pair + 1) * MXU_COLS)
                y = jnp.dot(pooled_ref[slot, :, cs], wp_ref[pair], preferred_element_type=F32)
                for rs in _strips(CHUNK, STRIP):
                    rows = slice(r0 + rs.start, r0 + rs.stop)
                    zp = rest_r[rows, R_ZP + pair * MXU_COLS:R_ZP + (pair + 1) * MXU_COLS]
                    mix_ref[rows, cs] = (y[rs] * ls_ref[:, cs] * _silu(zp)).astype(BF16)

        def head_chunk(c, hd, slot):
            r0 = c * CHUNK
            hs = slice(hd * HEAD_DIM, (hd + 1) * HEAD_DIM)
            ks = slice(D_MLSTM + hd * HEAD_DIM, D_MLSTM + (hd + 1) * HEAD_DIM)

            def conv(cols_, rs):
                n = STRIP // SUBLANES
                base = CONV_HIST + r0 + rs.start - SUBLANES
                vs = [qk_r[base + SUBLANES * i:base + SUBLANES * (i + 1), cols_] for i in range(n + 1)]
                row = lax.broadcasted_iota(jnp.int32, (SUBLANES, HEAD_DIM), 0)
                w_now = cw_ref[CONV_WIDTH - 1:CONV_WIDTH, cols_]
                ys = [cb_ref[:, cols_] + w_now * vs[i + 1] for i in range(n)]
                for k in range(1, CONV_WIDTH):
                    w_k = cw_ref[CONV_WIDTH - 1 - k:CONV_WIDTH - k, cols_]
                    rolled = [pltpu.roll(v, k, axis=0) for v in vs]
                    ys = [ys[i] + w_k * jnp.where(row >= k, rolled[i + 1], rolled[i]) for i in range(n)]
                return _silu(jnp.concatenate(ys, axis=0))

            def gcol(col, rs):
                return gcol_r[c, rs, col + hd:col + hd + 1]

            for rs in _strips(CHUNK, STRIP):
                kf_ref[slot, rs, :] = conv(ks, rs) * (HEAD_DIM ** -0.5)
            yield
            kt = kf_ref[slot].T
            w_row = grow_r[c, G_W + hd:G_W + hd + 1, :]
            for rs in _strips(HEAD_DIM, STRIP):
                ktbf_ref[slot, rs, :] = kt[rs].astype(BF16)
                kwt_ref[slot, rs, :] = (kt[rs] * w_row).astype(BF16)
                rhs2_ref[slot, CHUNK + rs.start:CHUNK + rs.stop, :] = caug_ref[hd, rs, :].astype(BF16)
            yield
            for rs in _strips(CHUNK, STRIP):
                rows = slice(r0 + rs.start, r0 + rs.stop)
                q = conv(hs, rs)
                qbf_ref[slot, rs, :] = q.astype(BF16)
                lhs2_ref[slot, rs, CHUNK:] = (q * gcol(GC_INTER, rs)).astype(BF16)
                v = rest_r[rows, R_V + hd * HEAD_DIM:R_V + (hd + 1) * HEAD_DIM]
                rhs2_ref[slot, rs, 0:HEAD_DIM] = v.astype(BF16)
            yield
            sc = jnp.dot(qbf_ref[slot], ktbf_ref[slot], preferred_element_type=F32)
            upd = jnp.dot(kwt_ref[slot], rhs2_ref[slot, 0:CHUNK, :], preferred_element_type=F32)
            dec = grow_r[c, G_DECAY + hd:G_DECAY + hd + 1, :]
            dec2 = jnp.concatenate([dec, dec], axis=1)
            for rs in _strips(HEAD_DIM, STRIP):
                caug_ref[hd, rs, :] = dec2 * caug_ref[hd, rs, :] + upd[rs]
            yield
            a_row = grow_r[c, G_A + hd:G_A + hd + 1, :]
            for rs in _strips(CHUNK, STRIP):
                row_i = lax.broadcasted_iota(jnp.int32, (STRIP, CHUNK), 0) + rs.start
                col_i = lax.broadcasted_iota(jnp.int32, (STRIP, CHUNK), 1)
                dmat = jnp.where(col_i <= row_i, jnp.exp(a_row - gcol(GC_M, rs)), 0.0)
                lhs2_ref[slot, rs, 0:CHUNK] = (sc[rs] * dmat).astype(BF16)
            yield
            numden = jnp.dot(lhs2_ref[slot], rhs2_ref[slot], preferred_element_type=F32)
            yield
            for rs in _strips(CHUNK, STRIP):
                rows = slice(r0 + rs.start, r0 + rs.stop)
                num = numden[rs, :HEAD_DIM]
                den = numden[rs, HEAD_DIM:]
                hm = num / jnp.maximum(jnp.abs(den), gcol(GC_EFLOOR, rs))
                ms = jnp.mean(hm * hm, axis=-1, keepdims=True)
                og = jax.nn.sigmoid(rest_r[rows, R_O + hd * HEAD_DIM:R_O + (hd + 1) * HEAD_DIM])
                zm = rest_r[rows, R_ZM + hd * HEAD_DIM:R_ZM + (hd + 1) * HEAD_DIM]
                hm = hm * lax.rsqrt(ms + EPS) * mg_ref[:, hs]
                mix_ref[rows, D_POOL + hd * HEAD_DIM:D_POOL + (hd + 1) * HEAD_DIM] = (
                    hm * og * _silu(zm)).astype(BF16)

        def out_block(rb):
            rows = slice(rb * OUT_ROWS, (rb + 1) * OUT_ROWS)
            y = jnp.dot(mix_ref[rows, :], wout_ref[...], preferred_element_type=F32)
            o_ref[rows, :] = xres_ref[rows, :] + gate * y
            for rs in _strips(OUT_ROWS, NORM_STRIP):
                rows = slice(rb * OUT_ROWS + rs.start, rb * OUT_ROWS + rs.stop)
                z = o_ref[rows, :]
                ms = jnp.mean(z * z, axis=-1, keepdims=True)
                o_ref[rows, :] = z * lax.rsqrt(ms + EPS) * fg_ref[...]

        proj_fill = [lambda p=p, rb=rb: proj_piece(p, rb)
                     for p in _proj_pieces() for rb in range(T // PROJ_ROWS)]
        pools = [pool_chunk(c, c) for c in range(N_CHUNKS)]
        pool_fill = [lambda c=c: next(pools[c], None) for c in POOL_ORDER]
        fill = _merge_evenly([proj_fill, pool_fill])
        norm_fill = [lambda i=i: norm_block(i) for i in range(T // NORM_ROWS)]

        def emit(work, n):
            for _ in range(n):
                if work:
                    work.pop(0)()

        def advance(gen):
            try:
                next(gen)
                return True
            except StopIteration:
                return False

        units = [head_chunk(c, hd, c * N_HEADS + hd)
                 for c in range(N_CHUNKS) for hd in range(N_HEADS)]
        n_rounds = ROUNDS_PER_TILE
        len_fill = len(fill)
        gates = gates_tile()
        advance(gates)
        active = []
        rounds = 0
        while units or active:
            while units and len(active) < IN_FLIGHT:
                active.append(units.pop(0))
            active = [gen for gen in active if advance(gen)]
            rounds += 1
            emit(fill, (rounds * len_fill) // n_rounds - ((rounds - 1) * len_fill) // n_rounds)
            advance(gates)
        for _ in gates:
            pass
        emit(fill, len(fill))

        u_w[0:POOL_HIST, :] = u_r[T:T + POOL_HIST, :]
        qk_w[0:CONV_HIST, :] = qk_r[T:T + CONV_HIST, :]

        for rb in range(T // OUT_ROWS):
            emit(norm_fill, len(norm_fill) // (T // OUT_ROWS))
            out_block(rb)
        emit(norm_fill, len(norm_fill))

    return block_kernel


def _const_spec(shape):
    return pl.BlockSpec(shape, lambda s: (0,) * len(shape), pipeline_mode=pl.Buffered(1))


@jax.jit
def kernel(x, c, norm_g, w_ada, b_ada, w_in, b_gates, conv_w, conv_b, w_pool, ls_pool,
           mh_norm_g, w_out, final_g):
    B, S, D = x.shape
    T = SEQ_TILE
    assert D == D_MODEL and S % T == 0
    assert w_in.shape == (1, D_MODEL, OFF_G + 2 * N_HEADS)
    tiles_per_seq = S // T
    n_tiles = B * tiles_per_seq

    n_ada = 3
    mod = pl.pallas_call(
        _ada_kernel,
        grid=(n_ada,),
        in_specs=[pl.BlockSpec((B, D), lambda n: (0, 0)),
                  pl.BlockSpec((D, D), lambda n: (0, n)),
                  pl.BlockSpec((1, D), lambda n: (0, n))],
        out_specs=pl.BlockSpec((B, D), lambda n: (0, n)),
        out_shape=jax.ShapeDtypeStruct((B, n_ada * D), F32),
    )(c, w_ada[0], b_ada)
    mod = mod.reshape(B, n_ada, D)

    pad = GATE_PAD - 2 * N_HEADS
    w_in_p = jnp.pad(w_in[0], ((0, 0), (0, pad))).astype(BF16)
    b_gates_p = jnp.pad(b_gates, ((0, 0), (0, pad)))
    n_pairs = len(POOL_WINDOWS) // 2
    wp = w_pool[0].astype(BF16).reshape(n_pairs, 2, POOL_GROUP_DIM, POOL_GROUP_DIM)
    zeros = jnp.zeros((n_pairs, POOL_GROUP_DIM, POOL_GROUP_DIM), BF16)
    wp_bd = jnp.concatenate([jnp.concatenate([wp[:, 0], zeros], axis=2),
                             jnp.concatenate([zeros, wp[:, 1]], axis=2)], axis=1)
    x_tiles = x.reshape(n_tiles, T, D)

    def tile_n(s):
        return (jnp.minimum(s, n_tiles - 1), 0, 0)

    def tile_c(s):
        return (jnp.clip(s - (PIPE_DEPTH - 1), 0, n_tiles - 1), 0, 0)

    out = pl.pallas_call(
        _make_block_kernel(tiles_per_seq, n_tiles),
        grid=(n_tiles + PIPE_DEPTH - 1,),
        in_specs=[
            pl.BlockSpec((None, T, D), tile_n),
            pl.BlockSpec((None, T, D), tile_c),
            _const_spec((B, n_ada, D)),
            _const_spec((1, D)),
            _const_spec((D, D_PROJ_PAD)),
            _const_spec((1, GATE_PAD)),
            _const_spec((CONV_WIDTH, 2 * D_MLSTM)),
            _const_spec((1, 2 * D_MLSTM)),
            _const_spec((n_pairs, MXU_COLS, MXU_COLS)),
            _const_spec((1, D_POOL)),
            _const_spec((1, D_MLSTM)),
            _const_spec((D, D)),
            _const_spec((1, D)),
        ],
        out_specs=pl.BlockSpec((None, T, D), tile_c),
        out_shape=jax.ShapeDtypeStruct((n_tiles, T, D), x.dtype),
        scratch_shapes=[
            *2 * [pltpu.VMEM((T, D), BF16)],
            *2 * [pltpu.VMEM((POOL_HIST + T, D_POOL), BF16)],
            *2 * [pltpu.VMEM((CONV_HIST + T, 2 * D_MLSTM), F32)],
            *2 * [pltpu.VMEM((T, R_COLS), F32)],
            *2 * [pltpu.VMEM((N_CHUNKS, CHUNK, CHUNK), F32)],
            *2 * [pltpu.VMEM((N_CHUNKS, G_ROWS, CHUNK), F32)],
            pltpu.VMEM((len(POOL_WINDOWS), CHUNK, 2 * CHUNK), BF16),
            pltpu.VMEM((T, D), BF16),
            pltpu.VMEM((N_HEADS, HEAD_DIM, 2 * HEAD_DIM), F32),
            pltpu.VMEM((SUBLANES, CHUNK), F32),
            pltpu.VMEM((N_CHUNKS, CHUNK, D_POOL), BF16),
            pltpu.VMEM((N_HEAD_UNITS, CHUNK, HEAD_DIM), F32),
            pltpu.VMEM((N_HEAD_UNITS, HEAD_DIM, CHUNK), BF16),
            pltpu.VMEM((N_HEAD_UNITS, HEAD_DIM, CHUNK), BF16),
            pltpu.VMEM((N_HEAD_UNITS, CHUNK, HEAD_DIM), BF16),
            pltpu.VMEM((N_HEAD_UNITS, CHUNK, CHUNK + HEAD_DIM), BF16),
            pltpu.VMEM((N_HEAD_UNITS, CHUNK + HEAD_DIM, 2 * HEAD_DIM), BF16),
        ],
        compiler_params=pltpu.CompilerParams(
            dimension_semantics=("arbitrary",),
            vmem_limit_bytes=VMEM_LIMIT),
    )(x_tiles, x_tiles, mod, norm_g, w_in_p, b_gates_p, conv_w[0], conv_b,
      wp_bd, ls_pool, mh_norm_g, w_out[0].astype(BF16), final_g.reshape(1, D))
    return out.reshape(B, S, D)
```

```python
import jax
import jax.numpy as jnp
from jax import lax
from jax.experimental import pallas as pl
from jax.experimental.pallas import tpu as pltpu

F32 = jnp.float32
BF16 = jnp.bfloat16

D_MODEL = 1024
D_POOL = 512
D_MLSTM = 512
POOL_WINDOWS = (2, 4, 8, 16)
POOL_GROUP_DIM = 128
N_HEADS = 4
HEAD_DIM = 128
CONV_WIDTH = 4
CHUNK = 128
EPS = 1e-6

LANES = 128
SUBLANES = 8
MXU_COLS = 256
POOL_HIST = CHUNK
POOL_FIX_ROWS = 16
CONV_HIST = SUBLANES
GATE_PAD = LANES

OFF_U = 0
OFF_ZP = 512
OFF_QK = 1024
OFF_V = 2048
OFF_G = 3584
D_PROJ_PAD = OFF_G + GATE_PAD

R_ZP = 0
R_V = 512
R_O = 1024
R_ZM = 1536
R_COLS = 2048

G_A = 0
G_W = SUBLANES
G_DECAY = 2 * SUBLANES
G_ROWS = 3 * SUBLANES

GC_M = 0
GC_INTER = SUBLANES
GC_EFLOOR = 2 * SUBLANES

SEQ_TILE = 512
N_CHUNKS = SEQ_TILE // CHUNK
STRIP = 128
NORM_STRIP = 16
NORM_ROWS = 64
OUT_ROWS = 256
PROJ_ROWS = 256
PROJ_COLS = 2 * MXU_COLS
PIPE_DEPTH = 3
N_PAIRED = 6
N_HEAD_UNITS = N_CHUNKS * N_HEADS
IN_FLIGHT = 3
ROUNDS_PER_TILE = 44
POOL_ORDER = (0, 1, 0, 2, 1, 3, 2, 3)
VMEM_LIMIT = 56 * 1024 * 1024


def _silu(z):
    return z * jax.nn.sigmoid(z)


def _log_sigmoid(z):
    return jnp.minimum(z, 0.0) - jnp.log1p(jnp.exp(-jnp.abs(z)))


def _strips(n, step):
    return [slice(r, r + step) for r in range(0, n, step)]


def _merge_evenly(lists):
    merged = []
    total = sum(len(l) for l in lists)
    taken = [0] * len(lists)
    for i in range(1, total + 1):
        for j, l in enumerate(lists):
            while taken[j] < (i * len(l)) // total:
                merged.append(l[taken[j]])
                taken[j] += 1
    return merged


def _ada_kernel(c_ref, w_ref, b_ref, o_ref):
    c = c_ref[...]
    o_ref[...] = jnp.dot(_silu(c), w_ref[...], preferred_element_type=F32) + b_ref[...]


def _proj_pieces():
    pieces = []
    for lo in range(OFF_U, OFF_ZP, PROJ_COLS):
        pieces.append((lo, PROJ_COLS, "u", lo - OFF_U))
    for lo in range(OFF_ZP, OFF_QK, PROJ_COLS):
        pieces.append((lo, PROJ_COLS, "rest", R_ZP + lo - OFF_ZP))
    for lo in range(OFF_QK, OFF_V, PROJ_COLS):
        pieces.append((lo, PROJ_COLS, "qk", lo - OFF_QK))
    for lo in range(OFF_V, OFF_G, PROJ_COLS):
        pieces.append((lo, PROJ_COLS, "rest", R_V + lo - OFF_V))
    return pieces


def _make_block_kernel(tiles_per_seq, n_tiles):
    T = SEQ_TILE

    def block_kernel(x_ref, xres_ref, mod_ref, ng_ref, win_ref, bg_ref, cw_ref, cb_ref, wp_ref,
                     ls_ref, mg_ref, wout_ref, fg_ref, o_ref, *scratch):
        s = pl.program_id(0)
        pairs, shared = scratch[:2 * N_PAIRED], scratch[2 * N_PAIRED:]
        band_ref, caug_ref, m_ref, rhs2_ref = shared[0], shared[2], shared[3], shared[-1]

        @pl.when(s == 0)
        def _():
            for ref in pairs + (caug_ref, m_ref):
                ref[...] = jnp.zeros(ref.shape, ref.dtype)
            rhs2_ref[:, 0:CHUNK, HEAD_DIM:] = jnp.ones((N_HEAD_UNITS, CHUNK, HEAD_DIM), BF16)
            t_i = lax.broadcasted_iota(jnp.int32, (CHUNK, 2 * CHUNK), 0)
            r_i = lax.broadcasted_iota(jnp.int32, (CHUNK, 2 * CHUNK), 1) - POOL_HIST
            for g, w in enumerate(POOL_WINDOWS):
                in_window = (r_i <= t_i) & (r_i > t_i - w)
                band = jnp.where(in_window, 1.0 / w, 0.0) - jnp.where(r_i == t_i, 1.0, 0.0)
                band_ref[g] = band.astype(BF16)

        inputs = (x_ref, xres_ref, mod_ref, ng_ref, win_ref, bg_ref, cw_ref, cb_ref, wp_ref,
                  ls_ref, mg_ref, wout_ref, fg_ref, o_ref)
        for parity in range(2):
            written = pairs[parity::2]
            read = pairs[1 - parity::2]
            bufs = tuple(r for wr in zip(written, read) for r in wr)

            @pl.when(s % 2 == parity)
            def _(bufs=bufs):
                step(s, *inputs, *bufs, *shared)

    def step(s, x_ref, xres_ref, mod_ref, ng_ref, win_ref, bg_ref, cw_ref, cb_ref, wp_ref,
             ls_ref, mg_ref, wout_ref, fg_ref, o_ref,
             h_w, h_r, u_w, u_r, qk_w, qk_r, rest_w, rest_r, gcol_w, gcol_r, grow_w, grow_r,
             band_ref, mix_ref, caug_ref, m_ref, pooled_ref, kf_ref, ktbf_ref, kwt_ref, qbf_ref,
             lhs2_ref, rhs2_ref):
        tile_n = jnp.minimum(s, n_tiles - 1)
        tile_c = jnp.clip(s - (PIPE_DEPTH - 1), 0, n_tiles - 1)

        first_of_seq = (s - (PIPE_DEPTH - 1)) % tiles_per_seq == 0
        first_of_seq_g = (s - 1) % tiles_per_seq == 0

        @pl.when(first_of_seq)
        def _():
            u_r[0:POOL_HIST, :] = jnp.zeros((POOL_HIST, D_POOL), BF16)
            qk_r[0:CONV_HIST, :] = jnp.zeros((CONV_HIST, 2 * D_MLSTM), F32)
            caug_ref[...] = jnp.zeros(caug_ref.shape, F32)

        mod_n = mod_ref[tile_n // tiles_per_seq]
        shift = mod_n[0:1, :]
        gscale = ng_ref[...] * (1.0 + mod_n[1:2, :])
        gate = mod_ref[tile_c // tiles_per_seq][2:3, :]

        def norm_block(i):
            for rs in _strips(NORM_ROWS, NORM_STRIP):
                rows = slice(i * NORM_ROWS + rs.start, i * NORM_ROWS + rs.stop)
                x = x_ref[rows, :]
                ms = jnp.mean(x * x, axis=-1, keepdims=True)
                h_w[rows, :] = (x * lax.rsqrt(ms + EPS) * gscale + shift).astype(BF16)

        def proj_piece(piece, rb):
            lo, width, dst, col = piece
            r0, r1 = rb * PROJ_ROWS, (rb + 1) * PROJ_ROWS
            y = jnp.dot(h_r[r0:r1, :], win_ref[:, lo:lo + width],
                        preferred_element_type=F32)
            if dst == "u":
                u_w[POOL_HIST + r0:POOL_HIST + r1, col:col + width] = y.astype(BF16)
            elif dst == "qk":
                qk_w[CONV_HIST + r0:CONV_HIST + r1, col:col + width] = y
            else:
                rest_w[r0:r1, col:col + width] = y

        zero_rows = jnp.zeros((CHUNK - 3 * SUBLANES, CHUNK), F32)

        def gates_tile():
            gts = []
            for rb in range(T // PROJ_ROWS):
                g_pre = jnp.dot(h_r[rb * PROJ_ROWS:(rb + 1) * PROJ_ROWS, :],
                                win_ref[:, OFF_G:D_PROJ_PAD],
                                preferred_element_type=F32) + bg_ref[...]
                for cc in range(PROJ_ROWS // CHUNK):
                    gts.append(g_pre[cc * CHUNK:(cc + 1) * CHUNK, :].T[0:SUBLANES, :])
            yield
            lane = lax.broadcasted_iota(jnp.int32, (SUBLANES, CHUNK), 1)
            bs = [pltpu.roll(_log_sigmoid(gt), N_HEADS, axis=0) for gt in gts]
            k = 1
            while k < CHUNK:
                bs = [x + jnp.where(lane >= k, pltpu.roll(x, k, axis=1), 0.0) for x in bs]
                k *= 2
                yield
            a_s = [gt - b for gt, b in zip(gts, bs)]
            cm = a_s
            k = 1
            while k < CHUNK:
                cm = [jnp.maximum(x, jnp.where(lane >= k, pltpu.roll(x, k, axis=1), -jnp.inf))
                      for x in cm]
                k *= 2
                yield
            m_run = jnp.where(first_of_seq_g, 0.0, m_ref[...])
            for c in range(N_CHUNKS):
                a, b = a_s[c], bs[c]
                big_m = jnp.maximum(m_run, cm[c])
                m_last = jnp.broadcast_to(big_m[:, CHUNK - 1:CHUNK], big_m.shape)
                b_last = jnp.broadcast_to(b[:, CHUNK - 1:CHUNK], b.shape)
                yield
                inter = jnp.exp(m_run - big_m)
                efloor = jnp.exp(-(b + big_m))
                grow_w[c, G_A:G_A + SUBLANES, :] = a
                grow_w[c, G_W:G_W + SUBLANES, :] = jnp.exp(a - m_last)
                grow_w[c, G_DECAY:G_DECAY + SUBLANES, :] = jnp.exp(m_run - m_last)
                gcol_w[c] = jnp.concatenate([big_m, inter, efloor, zero_rows], axis=0).T
                m_run = b_last + m_last
            m_ref[...] = m_run

        def pool_chunk(c, slot):
            r0 = c * CHUNK
            for g, w in enumerate(POOL_WINDOWS):
                cs = slice(g * POOL_GROUP_DIM, (g + 1) * POOL_GROUP_DIM)
                pg = jnp.dot(band_ref[g], u_r[r0:r0 + 2 * CHUNK, cs],
                             preferred_element_type=F32)
                if c == 0:
                    cur = u_r[POOL_HIST:POOL_HIST + POOL_FIX_ROWS, cs].astype(F32)
                    pos = lax.broadcasted_iota(jnp.int32, (POOL_FIX_ROWS, POOL_GROUP_DIM), 0) + 1
                    short = float(w) / jnp.minimum(pos, w).astype(F32) - 1.0
                    scale = 1.0 + first_of_seq.astype(F32) * short
                    fixed = (pg[0:POOL_FIX_ROWS] + cur) * scale - cur
                    pooled_ref[slot, 0:POOL_FIX_ROWS, cs] = fixed.astype(BF16)
                    pooled_ref[slot, POOL_FIX_ROWS:, cs] = pg[POOL_FIX_ROWS:].astype(BF16)
                else:
                    pooled_ref[slot, :, cs] = pg.astype(BF16)
            yield
            for pair in range(len(POOL_WINDOWS) // 2):
                cs = slice(pair * MXU_COLS, (pair + 1) * MXU_COLS)
                y = jnp.dot(pooled_ref[slot, :, cs], wp_ref[pair], preferred_element_type=F32)
                for rs in _strips(CHUNK, STRIP):
                    rows = slice(r0 + rs.start, r0 + rs.stop)
                    zp = rest_r[rows, R_ZP + pair * MXU_COLS:R_ZP + (pair + 1) * MXU_COLS]
                    mix_ref[rows, cs] = (y[rs] * ls_ref[:, cs] * _silu(zp)).astype(BF16)

        def head_chunk(c, hd, slot):
            r0 = c * CHUNK
            hs = slice(hd * HEAD_DIM, (hd + 1) * HEAD_DIM)
            ks = slice(D_MLSTM + hd * HEAD_DIM, D_MLSTM + (hd + 1) * HEAD_DIM)

            def conv(cols_, rs):
                n = STRIP // SUBLANES
                base = CONV_HIST + r0 + rs.start - SUBLANES
                vs = [qk_r[base + SUBLANES * i:base + SUBLANES * (i + 1), cols_] for i in range(n + 1)]
                row = lax.broadcasted_iota(jnp.int32, (SUBLANES, HEAD_DIM), 0)
                w_now = cw_ref[CONV_WIDTH - 1:CONV_WIDTH, cols_]
                ys = [cb_ref[:, cols_] + w_now * vs[i + 1] for i in range(n)]
                for k in range(1, CONV_WIDTH):
                    w_k = cw_ref[CONV_WIDTH - 1 - k:CONV_WIDTH - k, cols_]
                    rolled = [pltpu.roll(v, k, axis=0) for v in vs]
                    ys = [ys[i] + w_k * jnp.where(row >= k, rolled[i + 1], rolled[i]) for i in range(n)]
                return _silu(jnp.concatenate(ys, axis=0))

            def gcol(col, rs):
                return gcol_r[c, rs, col + hd:col + hd + 1]

            for rs in _strips(CHUNK, STRIP):
                kf_ref[slot, rs, :] = conv(ks, rs) * (HEAD_DIM ** -0.5)
            yield
            kt = kf_ref[slot].T
            w_row = grow_r[c, G_W + hd:G_W + hd + 1, :]
            for rs in _strips(HEAD_DIM, STRIP):
                ktbf_ref[slot, rs, :] = kt[rs].astype(BF16)
                kwt_ref[slot, rs, :] = (kt[rs] * w_row).astype(BF16)
                rhs2_ref[slot, CHUNK + rs.start:CHUNK + rs.stop, :] = caug_ref[hd, rs, :].astype(BF16)
            yield
            for rs in _strips(CHUNK, STRIP):
                rows = slice(r0 + rs.start, r0 + rs.stop)
                q = conv(hs, rs)
                qbf_ref[slot, rs, :] = q.astype(BF16)
                lhs2_ref[slot, rs, CHUNK:] = (q * gcol(GC_INTER, rs)).astype(BF16)
                v = rest_r[rows, R_V + hd * HEAD_DIM:R_V + (hd + 1) * HEAD_DIM]
                rhs2_ref[slot, rs, 0:HEAD_DIM] = v.astype(BF16)
            yield
            sc = jnp.dot(qbf_ref[slot], ktbf_ref[slot], preferred_element_type=F32)
            upd = jnp.dot(kwt_ref[slot], rhs2_ref[slot, 0:CHUNK, :], preferred_element_type=F32)
            dec = grow_r[c, G_DECAY + hd:G_DECAY + hd + 1, :]
            dec2 = jnp.concatenate([dec, dec], axis=1)
            for rs in _strips(HEAD_DIM, STRIP):
                caug_ref[hd, rs, :] = dec2 * caug_ref[hd, rs, :] + upd[rs]
            yield
            a_row = grow_r[c, G_A + hd:G_A + hd + 1, :]
            for rs in _strips(CHUNK, STRIP):
                row_i = lax.broadcasted_iota(jnp.int32, (STRIP, CHUNK), 0) + rs.start
                col_i = lax.broadcasted_iota(jnp.int32, (STRIP, CHUNK), 1)
                dmat = jnp.where(col_i <= row_i, jnp.exp(a_row - gcol(GC_M, rs)), 0.0)
                lhs2_ref[slot, rs, 0:CHUNK] = (sc[rs] * dmat).astype(BF16)
            yield
            numden = jnp.dot(lhs2_ref[slot], rhs2_ref[slot], preferred_element_type=F32)
            yield
            for rs in _strips(CHUNK, STRIP):
                rows = slice(r0 + rs.start, r0 + rs.stop)
                num = numden[rs, :HEAD_DIM]
                den = numden[rs, HEAD_DIM:]
                hm = num / jnp.maximum(jnp.abs(den), gcol(GC_EFLOOR, rs))
                ms = jnp.mean(hm * hm, axis=-1, keepdims=True)
                og = jax.nn.sigmoid(rest_r[rows, R_O + hd * HEAD_DIM:R_O + (hd + 1) * HEAD_DIM])
                zm = rest_r[rows, R_ZM + hd * HEAD_DIM:R_ZM + (hd + 1) * HEAD_DIM]
                hm = hm * lax.rsqrt(ms + EPS) * mg_ref[:, hs]
                mix_ref[rows, D_POOL + hd * HEAD_DIM:D_POOL + (hd + 1) * HEAD_DIM] = (
                    hm * og * _silu(zm)).astype(BF16)

        def out_block(rb):
            rows = slice(rb * OUT_ROWS, (rb + 1) * OUT_ROWS)
            y = jnp.dot(mix_ref[rows, :], wout_ref[...], preferred_element_type=F32)
            o_ref[rows, :] = xres_ref[rows, :] + gate * y
            for rs in _strips(OUT_ROWS, NORM_STRIP):
                rows = slice(rb * OUT_ROWS + rs.start, rb * OUT_ROWS + rs.stop)
                z = o_ref[rows, :]
                ms = jnp.mean(z * z, axis=-1, keepdims=True)
                o_ref[rows, :] = z * lax.rsqrt(ms + EPS) * fg_ref[...]

        proj_fill = [lambda p=p, rb=rb: proj_piece(p, rb)
                     for p in _proj_pieces() for rb in range(T // PROJ_ROWS)]
        pools = [pool_chunk(c, c) for c in range(N_CHUNKS)]
        pool_fill = [lambda c=c: next(pools[c], None) for c in POOL_ORDER]
        fill = _merge_evenly([proj_fill, pool_fill])
        norm_fill = [lambda i=i: norm_block(i) for i in range(T // NORM_ROWS)]

        def emit(work, n):
            for _ in range(n):
                if work:
                    work.pop(0)()

        def advance(gen):
            try:
                next(gen)
                return True
            except StopIteration:
                return False

        units = [head_chunk(c, hd, c * N_HEADS + hd)
                 for c in range(N_CHUNKS) for hd in range(N_HEADS)]
        n_rounds = ROUNDS_PER_TILE
        len_fill = len(fill)
        gates = gates_tile()
        advance(gates)
        active = []
        rounds = 0
        while units or active:
            while units and len(active) < IN_FLIGHT:
                active.append(units.pop(0))
            active = [gen for gen in active if advance(gen)]
            rounds += 1
            emit(fill, (rounds * len_fill) // n_rounds - ((rounds - 1) * len_fill) // n_rounds)
            advance(gates)
        for _ in gates:
            pass
        emit(fill, len(fill))

        u_w[0:POOL_HIST, :] = u_r[T:T + POOL_HIST, :]
        qk_w[0:CONV_HIST, :] = qk_r[T:T + CONV_HIST, :]

        for rb in range(T // OUT_ROWS):
            emit(norm_fill, len(norm_fill) // (T // OUT_ROWS))
            out_block(rb)
        emit(norm_fill, len(norm_fill))

    return block_kernel


def _const_spec(shape):
    return pl.BlockSpec(shape, lambda s: (0,) * len(shape), pipeline_mode=pl.Buffered(1))


@jax.jit
def kernel(x, c, norm_g, w_ada, b_ada, w_in, b_gates, conv_w, conv_b, w_pool, ls_pool,
           mh_norm_g, w_out, final_g):
    B, S, D = x.shape
    T = SEQ_TILE
    assert D == D_MODEL and S % T == 0
    assert w_in.shape == (1, D_MODEL, OFF_G + 2 * N_HEADS)
    tiles_per_seq = S // T
    n_tiles = B * tiles_per_seq

    n_ada = 3
    mod = pl.pallas_call(
        _ada_kernel,
        grid=(n_ada,),
        in_specs=[pl.BlockSpec((B, D), lambda n: (0, 0)),
                  pl.BlockSpec((None, D, D), lambda n: (0, 0, n)),
                  pl.BlockSpec((1, D), lambda n: (0, n))],
        out_specs=pl.BlockSpec((B, D), lambda n: (0, n)),
        out_shape=jax.ShapeDtypeStruct((B, n_ada * D), F32),
    )(c, w_ada, b_ada)
    mod = mod.reshape(B, n_ada, D)

    pad = GATE_PAD - 2 * N_HEADS
    w_in_p = jnp.pad(w_in[0], ((0, 0), (0, pad))).astype(BF16)
    b_gates_p = jnp.pad(b_gates, ((0, 0), (0, pad)))
    n_pairs = len(POOL_WINDOWS) // 2
    wp = w_pool[0].astype(BF16).reshape(n_pairs, 2, POOL_GROUP_DIM, POOL_GROUP_DIM)
    zeros = jnp.zeros((n_pairs, POOL_GROUP_DIM, POOL_GROUP_DIM), BF16)
    wp_bd = jnp.concatenate([jnp.concatenate([wp[:, 0], zeros], axis=2),
                             jnp.concatenate([zeros, wp[:, 1]], axis=2)], axis=1)
    x_tiles = x.reshape(n_tiles, T, D)

    def tile_n(s):
        return (jnp.minimum(s, n_tiles - 1), 0, 0)

    def tile_c(s):
        return (jnp.clip(s - (PIPE_DEPTH - 1), 0, n_tiles - 1), 0, 0)

    out = pl.pallas_call(
        _make_block_kernel(tiles_per_seq, n_tiles),
        grid=(n_tiles + PIPE_DEPTH - 1,),
        in_specs=[
            pl.BlockSpec((None, T, D), tile_n),
            pl.BlockSpec((None, T, D), tile_c),
            _const_spec((B, n_ada, D)),
            _const_spec((1, D)),
            _const_spec((D, D_PROJ_PAD)),
            _const_spec((1, GATE_PAD)),
            _const_spec((CONV_WIDTH, 2 * D_MLSTM)),
            _const_spec((1, 2 * D_MLSTM)),
            _const_spec((n_pairs, MXU_COLS, MXU_COLS)),
            _const_spec((1, D_POOL)),
            _const_spec((1, D_MLSTM)),
            _const_spec((D, D)),
            _const_spec((1, D)),
        ],
        out_specs=pl.BlockSpec((None, T, D), tile_c),
        out_shape=jax.ShapeDtypeStruct((n_tiles, T, D), x.dtype),
        scratch_shapes=[
            *2 * [pltpu.VMEM((T, D), BF16)],
            *2 * [pltpu.VMEM((POOL_HIST + T, D_POOL), BF16)],
            *2 * [pltpu.VMEM((CONV_HIST + T, 2 * D_MLSTM), F32)],
            *2 * [pltpu.VMEM((T, R_COLS), F32)],
            *2 * [pltpu.VMEM((N_CHUNKS, CHUNK, CHUNK), F32)],
            *2 * [pltpu.VMEM((N_CHUNKS, G_ROWS, CHUNK), F32)],
            pltpu.VMEM((len(POOL_WINDOWS), CHUNK, 2 * CHUNK), BF16),
            pltpu.VMEM((T, D), BF16),
            pltpu.VMEM((N_HEADS, HEAD_DIM, 2 * HEAD_DIM), F32),
            pltpu.VMEM((SUBLANES, CHUNK), F32),
            pltpu.VMEM((N_CHUNKS, CHUNK, D_POOL), BF16),
            pltpu.VMEM((N_HEAD_UNITS, CHUNK, HEAD_DIM), F32),
            pltpu.VMEM((N_HEAD_UNITS, HEAD_DIM, CHUNK), BF16),
            pltpu.VMEM((N_HEAD_UNITS, HEAD_DIM, CHUNK), BF16),
            pltpu.VMEM((N_HEAD_UNITS, CHUNK, HEAD_DIM), BF16),
            pltpu.VMEM((N_HEAD_UNITS, CHUNK, CHUNK + HEAD_DIM), BF16),
            pltpu.VMEM((N_HEAD_UNITS, CHUNK + HEAD_DIM, 2 * HEAD_DIM), BF16),
        ],
        compiler_params=pltpu.CompilerParams(
            dimension_semantics=("arbitrary",),
            vmem_limit_bytes=VMEM_LIMIT),
    )(x_tiles, x_tiles, mod, norm_g, w_in_p, b_gates_p, conv_w[0], conv_b,
      wp_bd, ls_pool, mh_norm_g, w_out[0].astype(BF16), final_g.reshape(1, D))
    return out.reshape(B, S, D)
```

```python
import jax
import jax.numpy as jnp
from jax import lax
from jax.experimental import pallas as pl
from jax.experimental.pallas import tpu as pltpu

F32 = jnp.float32
BF16 = jnp.bfloat16

D_MODEL = 1024
D_POOL = 512
D_MLSTM = 512
POOL_WINDOWS = (2, 4, 8, 16)
POOL_GROUP_DIM = 128
N_HEADS = 4
HEAD_DIM = 128
CONV_WIDTH = 4
CHUNK = 128
EPS = 1e-6

LANES = 128
SUBLANES = 8
MXU_COLS = 256
POOL_HIST = CHUNK
POOL_FIX_ROWS = 16
CONV_HIST = SUBLANES
GATE_PAD = LANES

OFF_U = 0
OFF_ZP = 512
OFF_QK = 1024
OFF_V = 2048
OFF_G = 3584

R_ZP = 0
R_V = 512
R_O = 1024
R_ZM = 1536
R_COLS = 2048

G_A = 0
G_W = SUBLANES
G_DECAY = 2 * SUBLANES
G_ROWS = 3 * SUBLANES

GC_M = 0
GC_INTER = SUBLANES
GC_EFLOOR = 2 * SUBLANES

SEQ_TILE = 512
N_CHUNKS = SEQ_TILE // CHUNK
STRIP = 128
NORM_STRIP = 16
NORM_ROWS = 64
OUT_ROWS = 256
PROJ_ROWS = 256
PROJ_COLS = 2 * MXU_COLS
PIPE_DEPTH = 3
N_PAIRED = 6
N_HEAD_UNITS = N_CHUNKS * N_HEADS
IN_FLIGHT = 3
ROUNDS_PER_TILE = 44
POOL_ORDER = (0, 1, 0, 2, 1, 3, 2, 3)
VMEM_LIMIT = 56 * 1024 * 1024


def _silu(z):
    return z * jax.nn.sigmoid(z)


def _log_sigmoid(z):
    return jnp.minimum(z, 0.0) - jnp.log1p(jnp.exp(-jnp.abs(z)))


def _strips(n, step):
    return [slice(r, r + step) for r in range(0, n, step)]


def _merge_evenly(lists):
    merged = []
    total = sum(len(l) for l in lists)
    taken = [0] * len(lists)
    for i in range(1, total + 1):
        for j, l in enumerate(lists):
            while taken[j] < (i * len(l)) // total:
                merged.append(l[taken[j]])
                taken[j] += 1
    return merged


def _ada_kernel(c_ref, w_ref, b_ref, o_ref):
    c = c_ref[...]
    o_ref[...] = jnp.dot(_silu(c), w_ref[...], preferred_element_type=F32) + b_ref[...]


def _proj_pieces():
    pieces = []
    for lo in range(OFF_U, OFF_ZP, PROJ_COLS):
        pieces.append((lo, PROJ_COLS, "u", lo - OFF_U))
    for lo in range(OFF_ZP, OFF_QK, PROJ_COLS):
        pieces.append((lo, PROJ_COLS, "rest", R_ZP + lo - OFF_ZP))
    for lo in range(OFF_QK, OFF_V, PROJ_COLS):
        pieces.append((lo, PROJ_COLS, "qk", lo - OFF_QK))
    for lo in range(OFF_V, OFF_G, PROJ_COLS):
        pieces.append((lo, PROJ_COLS, "rest", R_V + lo - OFF_V))
    return pieces


def _make_block_kernel(tiles_per_seq, n_tiles):
    T = SEQ_TILE

    def block_kernel(x_ref, xres_ref, mod_ref, ng_ref, win_ref, wg_ref, bg_ref, cw_ref, cb_ref, wp_ref,
                     ls_ref, mg_ref, wout_ref, fg_ref, o_ref, *scratch):
        s = pl.program_id(0)
        pairs, shared = scratch[:2 * N_PAIRED], scratch[2 * N_PAIRED:]
        band_ref, caug_ref, m_ref, rhs2_ref = shared[0], shared[2], shared[3], shared[-1]

        @pl.when(s == 0)
        def _():
            for ref in pairs + (caug_ref, m_ref):
                ref[...] = jnp.zeros(ref.shape, ref.dtype)
            rhs2_ref[:, 0:CHUNK, HEAD_DIM:] = jnp.ones((N_HEAD_UNITS, CHUNK, HEAD_DIM), BF16)
            t_i = lax.broadcasted_iota(jnp.int32, (CHUNK, 2 * CHUNK), 0)
            r_i = lax.broadcasted_iota(jnp.int32, (CHUNK, 2 * CHUNK), 1) - POOL_HIST
            for g, w in enumerate(POOL_WINDOWS):
                in_window = (r_i <= t_i) & (r_i > t_i - w)
                band = jnp.where(in_window, 1.0 / w, 0.0) - jnp.where(r_i == t_i, 1.0, 0.0)
                band_ref[g] = band.astype(BF16)

        inputs = (x_ref, xres_ref, mod_ref, ng_ref, win_ref, wg_ref, bg_ref, cw_ref, cb_ref, wp_ref,
                  ls_ref, mg_ref, wout_ref, fg_ref, o_ref)
        for parity in range(2):
            written = pairs[parity::2]
            read = pairs[1 - parity::2]
            bufs = tuple(r for wr in zip(written, read) for r in wr)

            @pl.when(s % 2 == parity)
            def _(bufs=bufs):
                step(s, *inputs, *bufs, *shared)

    def step(s, x_ref, xres_ref, mod_ref, ng_ref, win_ref, wg_ref, bg_ref, cw_ref, cb_ref, wp_ref,
             ls_ref, mg_ref, wout_ref, fg_ref, o_ref,
             h_w, h_r, u_w, u_r, qk_w, qk_r, rest_w, rest_r, gcol_w, gcol_r, grow_w, grow_r,
             band_ref, mix_ref, caug_ref, m_ref, pooled_ref, kf_ref, ktbf_ref, kwt_ref, qbf_ref,
             lhs2_ref, rhs2_ref):
        tile_n = jnp.minimum(s, n_tiles - 1)
        tile_c = jnp.clip(s - (PIPE_DEPTH - 1), 0, n_tiles - 1)

        first_of_seq = (s - (PIPE_DEPTH - 1)) % tiles_per_seq == 0
        first_of_seq_g = (s - 1) % tiles_per_seq == 0

        @pl.when(first_of_seq)
        def _():
            u_r[0:POOL_HIST, :] = jnp.zeros((POOL_HIST, D_POOL), BF16)
            qk_r[0:CONV_HIST, :] = jnp.zeros((CONV_HIST, 2 * D_MLSTM), F32)
            caug_ref[...] = jnp.zeros(caug_ref.shape, F32)

        mod_n = mod_ref[tile_n // tiles_per_seq]
        shift = mod_n[0:1, :]
        gscale = ng_ref[...] * (1.0 + mod_n[1:2, :])
        gate = mod_ref[tile_c // tiles_per_seq][2:3, :]

        def norm_block(i):
            for rs in _strips(NORM_ROWS, NORM_STRIP):
                rows = slice(i * NORM_ROWS + rs.start, i * NORM_ROWS + rs.stop)
                x = x_ref[rows, :]
                ms = jnp.mean(x * x, axis=-1, keepdims=True)
                h_w[rows, :] = (x * lax.rsqrt(ms + EPS) * gscale + shift).astype(BF16)

        def proj_piece(piece, rb):
            lo, width, dst, col = piece
            r0, r1 = rb * PROJ_ROWS, (rb + 1) * PROJ_ROWS
            y = jnp.dot(h_r[r0:r1, :], win_ref[:, lo:lo + width],
                        preferred_element_type=F32)
            if dst == "u":
                u_w[POOL_HIST + r0:POOL_HIST + r1, col:col + width] = y.astype(BF16)
            elif dst == "qk":
                qk_w[CONV_HIST + r0:CONV_HIST + r1, col:col + width] = y
            else:
                rest_w[r0:r1, col:col + width] = y

        zero_rows = jnp.zeros((CHUNK - 3 * SUBLANES, CHUNK), F32)

        def gates_tile():
            gts = []
            for rb in range(T // PROJ_ROWS):
                g_pre = jnp.dot(h_r[rb * PROJ_ROWS:(rb + 1) * PROJ_ROWS, :],
                                wg_ref[...],
                                preferred_element_type=F32) + bg_ref[...]
                for cc in range(PROJ_ROWS // CHUNK):
                    gts.append(g_pre[cc * CHUNK:(cc + 1) * CHUNK, :].T[0:SUBLANES, :])
            yield
            lane = lax.broadcasted_iota(jnp.int32, (SUBLANES, CHUNK), 1)
            bs = [pltpu.roll(_log_sigmoid(gt), N_HEADS, axis=0) for gt in gts]
            k = 1
            while k < CHUNK:
                bs = [x + jnp.where(lane >= k, pltpu.roll(x, k, axis=1), 0.0) for x in bs]
                k *= 2
                yield
            a_s = [gt - b for gt, b in zip(gts, bs)]
            cm = a_s
            k = 1
            while k < CHUNK:
                cm = [jnp.maximum(x, jnp.where(lane >= k, pltpu.roll(x, k, axis=1), -jnp.inf))
                      for x in cm]
                k *= 2
                yield
            m_run = jnp.where(first_of_seq_g, 0.0, m_ref[...])
            for c in range(N_CHUNKS):
                a, b = a_s[c], bs[c]
                big_m = jnp.maximum(m_run, cm[c])
                m_last = jnp.broadcast_to(big_m[:, CHUNK - 1:CHUNK], big_m.shape)
                b_last = jnp.broadcast_to(b[:, CHUNK - 1:CHUNK], b.shape)
                yield
                inter = jnp.exp(m_run - big_m)
                efloor = jnp.exp(-(b + big_m))
                grow_w[c, G_A:G_A + SUBLANES, :] = a
                grow_w[c, G_W:G_W + SUBLANES, :] = jnp.exp(a - m_last)
                grow_w[c, G_DECAY:G_DECAY + SUBLANES, :] = jnp.exp(m_run - m_last)
                gcol_w[c] = jnp.concatenate([big_m, inter, efloor, zero_rows], axis=0).T
                m_run = b_last + m_last
            m_ref[...] = m_run

        def pool_chunk(c, slot):
            r0 = c * CHUNK
            for g, w in enumerate(POOL_WINDOWS):
                cs = slice(g * POOL_GROUP_DIM, (g + 1) * POOL_GROUP_DIM)
                pg = jnp.dot(band_ref[g], u_r[r0:r0 + 2 * CHUNK, cs],
                             preferred_element_type=F32)
                if c == 0:
                    cur = u_r[POOL_HIST:POOL_HIST + POOL_FIX_ROWS, cs].astype(F32)
                    pos = lax.broadcasted_iota(jnp.int32, (POOL_FIX_ROWS, POOL_GROUP_DIM), 0) + 1
                    short = float(w) / jnp.minimum(pos, w).astype(F32) - 1.0
                    scale = 1.0 + first_of_seq.astype(F32) * short
                    fixed = (pg[0:POOL_FIX_ROWS] + cur) * scale - cur
                    pooled_ref[slot, 0:POOL_FIX_ROWS, cs] = fixed.astype(BF16)
                    pooled_ref[slot, POOL_FIX_ROWS:, cs] = pg[POOL_FIX_ROWS:].astype(BF16)
                else:
                    pooled_ref[slot, :, cs] = pg.astype(BF16)
            yield
            for pair in range(len(POOL_WINDOWS) // 2):
                cs = slice(pair * MXU_COLS, (pair + 1) * MXU_COLS)
                y = jnp.dot(pooled_ref[slot, :, cs], wp_ref[pair], preferred_element_type=F32)
                for rs in _strips(CHUNK, STRIP):
                    rows = slice(r0 + rs.start, r0 + rs.stop)
                    zp = rest_r[rows, R_ZP + pair * MXU_COLS:R_ZP + (pair + 1) * MXU_COLS]
                    mix_ref[rows, cs] = (y[rs] * ls_ref[:, cs] * _silu(zp)).astype(BF16)

        def head_chunk(c, hd, slot):
            r0 = c * CHUNK
            hs = slice(hd * HEAD_DIM, (hd + 1) * HEAD_DIM)
            ks = slice(D_MLSTM + hd * HEAD_DIM, D_MLSTM + (hd + 1) * HEAD_DIM)

            def conv(cols_, rs):
                n = STRIP // SUBLANES
                base = CONV_HIST + r0 + rs.start - SUBLANES
                vs = [qk_r[base + SUBLANES * i:base + SUBLANES * (i + 1), cols_] for i in range(n + 1)]
                row = lax.broadcasted_iota(jnp.int32, (SUBLANES, HEAD_DIM), 0)
                w_now = cw_ref[CONV_WIDTH - 1:CONV_WIDTH, cols_]
                ys = [cb_ref[:, cols_] + w_now * vs[i + 1] for i in range(n)]
                for k in range(1, CONV_WIDTH):
                    w_k = cw_ref[CONV_WIDTH - 1 - k:CONV_WIDTH - k, cols_]
                    rolled = [pltpu.roll(v, k, axis=0) for v in vs]
                    ys = [ys[i] + w_k * jnp.where(row >= k, rolled[i + 1], rolled[i]) for i in range(n)]
                return _silu(jnp.concatenate(ys, axis=0))

            def gcol(col, rs):
                return gcol_r[c, rs, col + hd:col + hd + 1]

            for rs in _strips(CHUNK, STRIP):
                kf_ref[slot, rs, :] = conv(ks, rs) * (HEAD_DIM ** -0.5)
            yield
            kt = kf_ref[slot].T
            w_row = grow_r[c, G_W + hd:G_W + hd + 1, :]
            for rs in _strips(HEAD_DIM, STRIP):
                ktbf_ref[slot, rs, :] = kt[rs].astype(BF16)
                kwt_ref[slot, rs, :] = (kt[rs] * w_row).astype(BF16)
                rhs2_ref[slot, CHUNK + rs.start:CHUNK + rs.stop, :] = caug_ref[hd, rs, :].astype(BF16)
            yield
            for rs in _strips(CHUNK, STRIP):
                rows = slice(r0 + rs.start, r0 + rs.stop)
                q = conv(hs, rs)
                qbf_ref[slot, rs, :] = q.astype(BF16)
                lhs2_ref[slot, rs, CHUNK:] = (q * gcol(GC_INTER, rs)).astype(BF16)
                v = rest_r[rows, R_V + hd * HEAD_DIM:R_V + (hd + 1) * HEAD_DIM]
                rhs2_ref[slot, rs, 0:HEAD_DIM] = v.astype(BF16)
            yield
            sc = jnp.dot(qbf_ref[slot], ktbf_ref[slot], preferred_element_type=F32)
            upd = jnp.dot(kwt_ref[slot], rhs2_ref[slot, 0:CHUNK, :], preferred_element_type=F32)
            dec = grow_r[c, G_DECAY + hd:G_DECAY + hd + 1, :]
            dec2 = jnp.concatenate([dec, dec], axis=1)
            for rs in _strips(HEAD_DIM, STRIP):
                caug_ref[hd, rs, :] = dec2 * caug_ref[hd, rs, :] + upd[rs]
            yield
            a_row = grow_r[c, G_A + hd:G_A + hd + 1, :]
            for rs in _strips(CHUNK, STRIP):
                row_i = lax.broadcasted_iota(jnp.int32, (STRIP, CHUNK), 0) + rs.start
                col_i = lax.broadcasted_iota(jnp.int32, (STRIP, CHUNK), 1)
                dmat = jnp.where(col_i <= row_i, jnp.exp(a_row - gcol(GC_M, rs)), 0.0)
                lhs2_ref[slot, rs, 0:CHUNK] = (sc[rs] * dmat).astype(BF16)
            yield
            numden = jnp.dot(lhs2_ref[slot], rhs2_ref[slot], preferred_element_type=F32)
            yield
            for rs in _strips(CHUNK, STRIP):
                rows = slice(r0 + rs.start, r0 + rs.stop)
                num = numden[rs, :HEAD_DIM]
                den = numden[rs, HEAD_DIM:]
                hm = num / jnp.maximum(jnp.abs(den), gcol(GC_EFLOOR, rs))
                ms = jnp.mean(hm * hm, axis=-1, keepdims=True)
                og = jax.nn.sigmoid(rest_r[rows, R_O + hd * HEAD_DIM:R_O + (hd + 1) * HEAD_DIM])
                zm = rest_r[rows, R_ZM + hd * HEAD_DIM:R_ZM + (hd + 1) * HEAD_DIM]
                hm = hm * lax.rsqrt(ms + EPS) * mg_ref[:, hs]
                mix_ref[rows, D_POOL + hd * HEAD_DIM:D_POOL + (hd + 1) * HEAD_DIM] = (
                    hm * og * _silu(zm)).astype(BF16)

        def out_block(rb):
            rows = slice(rb * OUT_ROWS, (rb + 1) * OUT_ROWS)
            y = jnp.dot(mix_ref[rows, :], wout_ref[...], preferred_element_type=F32)
            o_ref[rows, :] = xres_ref[rows, :] + gate * y
            for rs in _strips(OUT_ROWS, NORM_STRIP):
                rows = slice(rb * OUT_ROWS + rs.start, rb * OUT_ROWS + rs.stop)
                z = o_ref[rows, :]
                ms = jnp.mean(z * z, axis=-1, keepdims=True)
                o_ref[rows, :] = z * lax.rsqrt(ms + EPS) * fg_ref[...]

        proj_fill = [lambda p=p, rb=rb: proj_piece(p, rb)
                     for p in _proj_pieces() for rb in range(T // PROJ_ROWS)]
        pools = [pool_chunk(c, c) for c in range(N_CHUNKS)]
        pool_fill = [lambda c=c: next(pools[c], None) for c in POOL_ORDER]
        fill = _merge_evenly([proj_fill, pool_fill])
        norm_fill = [lambda i=i: norm_block(i) for i in range(T // NORM_ROWS)]

        def emit(work, n):
            for _ in range(n):
                if work:
                    work.pop(0)()

        def advance(gen):
            try:
                next(gen)
                return True
            except StopIteration:
                return False

        units = [head_chunk(c, hd, c * N_HEADS + hd)
                 for c in range(N_CHUNKS) for hd in range(N_HEADS)]
        n_rounds = ROUNDS_PER_TILE
        len_fill = len(fill)
        gates = gates_tile()
        advance(gates)
        active = []
        rounds = 0
        while units or active:
            while units and len(active) < IN_FLIGHT:
                active.append(units.pop(0))
            active = [gen for gen in active if advance(gen)]
            rounds += 1
            emit(fill, (rounds * len_fill) // n_rounds - ((rounds - 1) * len_fill) // n_rounds)
            advance(gates)
        for _ in gates:
            pass
        emit(fill, len(fill))

        u_w[0:POOL_HIST, :] = u_r[T:T + POOL_HIST, :]
        qk_w[0:CONV_HIST, :] = qk_r[T:T + CONV_HIST, :]

        for rb in range(T // OUT_ROWS):
            emit(norm_fill, len(norm_fill) // (T // OUT_ROWS))
            out_block(rb)
        emit(norm_fill, len(norm_fill))

    return block_kernel


def _const_spec(shape):
    return pl.BlockSpec(shape, lambda s: (0,) * len(shape), pipeline_mode=pl.Buffered(1))


@jax.jit
def kernel(x, c, norm_g, w_ada, b_ada, w_in, b_gates, conv_w, conv_b, w_pool, ls_pool,
           mh_norm_g, w_out, final_g):
    B, S, D = x.shape
    T = SEQ_TILE
    assert D == D_MODEL and S % T == 0
    assert w_in.shape == (1, D_MODEL, OFF_G + 2 * N_HEADS)
    tiles_per_seq = S // T
    n_tiles = B * tiles_per_seq

    n_ada = 3
    mod = pl.pallas_call(
        _ada_kernel,
        grid=(n_ada,),
        in_specs=[pl.BlockSpec((B, D), lambda n: (0, 0)),
                  pl.BlockSpec((None, D, D), lambda n: (0, 0, n)),
                  pl.BlockSpec((1, D), lambda n: (0, n))],
        out_specs=pl.BlockSpec((B, D), lambda n: (0, n)),
        out_shape=jax.ShapeDtypeStruct((B, n_ada * D), F32),
    )(c, w_ada, b_ada)
    mod = mod.reshape(B, n_ada, D)

    pad = GATE_PAD - 2 * N_HEADS
    w_main = w_in[0].astype(BF16)
    w_gates = jnp.pad(w_in[0][:, OFF_G:], ((0, 0), (0, pad))).astype(BF16)
    b_gates_p = jnp.pad(b_gates, ((0, 0), (0, pad)))
    n_pairs = len(POOL_WINDOWS) // 2
    wp = w_pool[0].astype(BF16).reshape(n_pairs, 2, POOL_GROUP_DIM, POOL_GROUP_DIM)
    zeros = jnp.zeros((n_pairs, POOL_GROUP_DIM, POOL_GROUP_DIM), BF16)
    wp_bd = jnp.concatenate([jnp.concatenate([wp[:, 0], zeros], axis=2),
                             jnp.concatenate([zeros, wp[:, 1]], axis=2)], axis=1)
    x_tiles = x.reshape(n_tiles, T, D)

    def tile_n(s):
        return (jnp.minimum(s, n_tiles - 1), 0, 0)

    def tile_c(s):
        return (jnp.clip(s - (PIPE_DEPTH - 1), 0, n_tiles - 1), 0, 0)

    out = pl.pallas_call(
        _make_block_kernel(tiles_per_seq, n_tiles),
        grid=(n_tiles + PIPE_DEPTH - 1,),
        in_specs=[
            pl.BlockSpec((None, T, D), tile_n),
            pl.BlockSpec((None, T, D), tile_c),
            _const_spec((B, n_ada, D)),
            _const_spec((1, D)),
            _const_spec((D, OFF_G + 2 * N_HEADS)),
            _const_spec((D, GATE_PAD)),
            _const_spec((1, GATE_PAD)),
            _const_spec((CONV_WIDTH, 2 * D_MLSTM)),
            _const_spec((1, 2 * D_MLSTM)),
            _const_spec((n_pairs, MXU_COLS, MXU_COLS)),
            _const_spec((1, D_POOL)),
            _const_spec((1, D_MLSTM)),
            _const_spec((D, D)),
            _const_spec((1, D)),
        ],
        out_specs=pl.BlockSpec((None, T, D), tile_c),
        out_shape=jax.ShapeDtypeStruct((n_tiles, T, D), x.dtype),
        scratch_shapes=[
            *2 * [pltpu.VMEM((T, D), BF16)],
            *2 * [pltpu.VMEM((POOL_HIST + T, D_POOL), BF16)],
            *2 * [pltpu.VMEM((CONV_HIST + T, 2 * D_MLSTM), F32)],
            *2 * [pltpu.VMEM((T, R_COLS), F32)],
            *2 * [pltpu.VMEM((N_CHUNKS, CHUNK, CHUNK), F32)],
            *2 * [pltpu.VMEM((N_CHUNKS, G_ROWS, CHUNK), F32)],
            pltpu.VMEM((len(POOL_WINDOWS), CHUNK, 2 * CHUNK), BF16),
            pltpu.VMEM((T, D), BF16),
            pltpu.VMEM((N_HEADS, HEAD_DIM, 2 * HEAD_DIM), F32),
            pltpu.VMEM((SUBLANES, CHUNK), F32),
            pltpu.VMEM((N_CHUNKS, CHUNK, D_POOL), BF16),
            pltpu.VMEM((N_HEAD_UNITS, CHUNK, HEAD_DIM), F32),
            pltpu.VMEM((N_HEAD_UNITS, HEAD_DIM, CHUNK), BF16),
            pltpu.VMEM((N_HEAD_UNITS, HEAD_DIM, CHUNK), BF16),
            pltpu.VMEM((N_HEAD_UNITS, CHUNK, HEAD_DIM), BF16),
            pltpu.VMEM((N_HEAD_UNITS, CHUNK, CHUNK + HEAD_DIM), BF16),
            pltpu.VMEM((N_HEAD_UNITS, CHUNK + HEAD_DIM, 2 * HEAD_DIM), BF16),
        ],
        compiler_params=pltpu.CompilerParams(
            dimension_semantics=("arbitrary",),
            vmem_limit_bytes=VMEM_LIMIT),
    )(x_tiles, x_tiles, mod, norm_g, w_main, w_gates, b_gates_p, conv_w[0], conv_b,
      wp_bd, ls_pool, mh_norm_g, w_out[0].astype(BF16), final_g.reshape(1, D))
    return out.reshape(B, S, D)
```

```python
import jax
import jax.numpy as jnp
from jax import lax
from jax.experimental import pallas as pl
from jax.experimental.pallas import tpu as pltpu

F32 = jnp.float32
BF16 = jnp.bfloat16

D_MODEL = 1024
D_POOL = 512
D_MLSTM = 512
POOL_WINDOWS = (2, 4, 8, 16)
POOL_GROUP_DIM = 128
N_HEADS = 4
HEAD_DIM = 128
CONV_WIDTH = 4
CHUNK = 128
EPS = 1e-6

LANES = 128
SUBLANES = 8
MXU_COLS = 256
POOL_HIST = CHUNK
POOL_FIX_ROWS = 16
CONV_HIST = SUBLANES
GATE_PAD = LANES

OFF_U = 0
OFF_ZP = 512
OFF_QK = 1024
OFF_V = 2048
OFF_G = 3584

R_ZP = 0
R_V = 512
R_O = 1024
R_ZM = 1536
R_COLS = 2048

G_A = 0
G_W = SUBLANES
G_DECAY = 2 * SUBLANES
G_ROWS = 3 * SUBLANES

GC_M = 0
GC_INTER = SUBLANES
GC_EFLOOR = 2 * SUBLANES

SEQ_TILE = 512
N_CHUNKS = SEQ_TILE // CHUNK
STRIP = 128
NORM_STRIP = 16
NORM_ROWS = 64
OUT_ROWS = 512
PROJ_ROWS = 256
PROJ_COLS = 2 * MXU_COLS
PIPE_DEPTH = 3
N_PAIRED = 6
N_HEAD_UNITS = N_CHUNKS * N_HEADS
IN_FLIGHT = 3
ROUNDS_PER_TILE = 44
POOL_ORDER = (0, 1, 0, 2, 1, 3, 2, 3)
VMEM_LIMIT = 56 * 1024 * 1024


def _silu(z):
    return z * jax.nn.sigmoid(z)


def _log_sigmoid(z):
    return jnp.minimum(z, 0.0) - jnp.log1p(jnp.exp(-jnp.abs(z)))


def _strips(n, step):
    return [slice(r, r + step) for r in range(0, n, step)]


def _merge_evenly(lists):
    merged = []
    total = sum(len(l) for l in lists)
    taken = [0] * len(lists)
    for i in range(1, total + 1):
        for j, l in enumerate(lists):
            while taken[j] < (i * len(l)) // total:
                merged.append(l[taken[j]])
                taken[j] += 1
    return merged


def _ada_kernel(c_ref, w_ref, b_ref, o_ref):
    c = c_ref[...]
    o_ref[...] = jnp.dot(_silu(c), w_ref[...], preferred_element_type=F32) + b_ref[...]


def _proj_pieces():
    pieces = []
    for lo in range(OFF_U, OFF_ZP, PROJ_COLS):
        pieces.append((lo, PROJ_COLS, "u", lo - OFF_U))
    for lo in range(OFF_ZP, OFF_QK, PROJ_COLS):
        pieces.append((lo, PROJ_COLS, "rest", R_ZP + lo - OFF_ZP))
    for lo in range(OFF_QK, OFF_V, PROJ_COLS):
        pieces.append((lo, PROJ_COLS, "qk", lo - OFF_QK))
    for lo in range(OFF_V, OFF_G, PROJ_COLS):
        pieces.append((lo, PROJ_COLS, "rest", R_V + lo - OFF_V))
    return pieces


def _make_block_kernel(tiles_per_seq, n_tiles):
    T = SEQ_TILE

    def block_kernel(x_ref, xres_ref, mod_ref, ng_ref, win_ref, wg_ref, bg_ref, cw_ref, cb_ref, wp_ref,
                     ls_ref, mg_ref, wout_ref, fg_ref, o_ref, *scratch):
        s = pl.program_id(0)
        pairs, shared = scratch[:2 * N_PAIRED], scratch[2 * N_PAIRED:]
        band_ref, caug_ref, m_ref, rhs2_ref = shared[0], shared[2], shared[3], shared[-1]

        @pl.when(s == 0)
        def _():
            for ref in pairs + (caug_ref, m_ref):
                ref[...] = jnp.zeros(ref.shape, ref.dtype)
            rhs2_ref[:, 0:CHUNK, HEAD_DIM:] = jnp.ones((N_HEAD_UNITS, CHUNK, HEAD_DIM), BF16)
            t_i = lax.broadcasted_iota(jnp.int32, (CHUNK, 2 * CHUNK), 0)
            r_i = lax.broadcasted_iota(jnp.int32, (CHUNK, 2 * CHUNK), 1) - POOL_HIST
            for g, w in enumerate(POOL_WINDOWS):
                in_window = (r_i <= t_i) & (r_i > t_i - w)
                band = jnp.where(in_window, 1.0 / w, 0.0) - jnp.where(r_i == t_i, 1.0, 0.0)
                band_ref[g] = band.astype(BF16)

        inputs = (x_ref, xres_ref, mod_ref, ng_ref, win_ref, wg_ref, bg_ref, cw_ref, cb_ref, wp_ref,
                  ls_ref, mg_ref, wout_ref, fg_ref, o_ref)
        for parity in range(2):
            written = pairs[parity::2]
            read = pairs[1 - parity::2]
            bufs = tuple(r for wr in zip(written, read) for r in wr)

            @pl.when(s % 2 == parity)
            def _(bufs=bufs):
                step(s, *inputs, *bufs, *shared)

    def step(s, x_ref, xres_ref, mod_ref, ng_ref, win_ref, wg_ref, bg_ref, cw_ref, cb_ref, wp_ref,
             ls_ref, mg_ref, wout_ref, fg_ref, o_ref,
             h_w, h_r, u_w, u_r, qk_w, qk_r, rest_w, rest_r, gcol_w, gcol_r, grow_w, grow_r,
             band_ref, mix_ref, caug_ref, m_ref, pooled_ref, kf_ref, ktbf_ref, kwt_ref, qbf_ref,
             lhs2_ref, rhs2_ref):
        tile_n = jnp.minimum(s, n_tiles - 1)
        tile_c = jnp.clip(s - (PIPE_DEPTH - 1), 0, n_tiles - 1)

        first_of_seq = (s - (PIPE_DEPTH - 1)) % tiles_per_seq == 0
        first_of_seq_g = (s - 1) % tiles_per_seq == 0

        @pl.when(first_of_seq)
        def _():
            u_r[0:POOL_HIST, :] = jnp.zeros((POOL_HIST, D_POOL), BF16)
            qk_r[0:CONV_HIST, :] = jnp.zeros((CONV_HIST, 2 * D_MLSTM), F32)
            caug_ref[...] = jnp.zeros(caug_ref.shape, F32)

        mod_n = mod_ref[tile_n // tiles_per_seq]
        shift = mod_n[0:1, :]
        gscale = ng_ref[...] * (1.0 + mod_n[1:2, :])
        gate = mod_ref[tile_c // tiles_per_seq][2:3, :]

        def norm_block(i):
            for rs in _strips(NORM_ROWS, NORM_STRIP):
                rows = slice(i * NORM_ROWS + rs.start, i * NORM_ROWS + rs.stop)
                x = x_ref[rows, :]
                ms = jnp.mean(x * x, axis=-1, keepdims=True)
                h_w[rows, :] = (x * lax.rsqrt(ms + EPS) * gscale + shift).astype(BF16)

        def proj_piece(piece, rb):
            lo, width, dst, col = piece
            r0, r1 = rb * PROJ_ROWS, (rb + 1) * PROJ_ROWS
            y = jnp.dot(h_r[r0:r1, :], win_ref[:, lo:lo + width],
                        preferred_element_type=F32)
            if dst == "u":
                u_w[POOL_HIST + r0:POOL_HIST + r1, col:col + width] = y.astype(BF16)
            elif dst == "qk":
                qk_w[CONV_HIST + r0:CONV_HIST + r1, col:col + width] = y
            else:
                rest_w[r0:r1, col:col + width] = y

        zero_rows = jnp.zeros((CHUNK - 3 * SUBLANES, CHUNK), F32)

        def gates_tile():
            gts = []
            for rb in range(T // PROJ_ROWS):
                g_pre = jnp.dot(h_r[rb * PROJ_ROWS:(rb + 1) * PROJ_ROWS, :],
                                wg_ref[...],
                                preferred_element_type=F32) + bg_ref[...]
                for cc in range(PROJ_ROWS // CHUNK):
                    gts.append(g_pre[cc * CHUNK:(cc + 1) * CHUNK, :].T[0:SUBLANES, :])
            yield
            lane = lax.broadcasted_iota(jnp.int32, (SUBLANES, CHUNK), 1)
            bs = [pltpu.roll(_log_sigmoid(gt), N_HEADS, axis=0) for gt in gts]
            k = 1
            while k < CHUNK:
                bs = [x + jnp.where(lane >= k, pltpu.roll(x, k, axis=1), 0.0) for x in bs]
                k *= 2
                yield
            a_s = [gt - b for gt, b in zip(gts, bs)]
            cm = a_s
            k = 1
            while k < CHUNK:
                cm = [jnp.maximum(x, jnp.where(lane >= k, pltpu.roll(x, k, axis=1), -jnp.inf))
                      for x in cm]
                k *= 2
                yield
            m_run = jnp.where(first_of_seq_g, 0.0, m_ref[...])
            for c in range(N_CHUNKS):
                a, b = a_s[c], bs[c]
                big_m = jnp.maximum(m_run, cm[c])
                m_last = jnp.broadcast_to(big_m[:, CHUNK - 1:CHUNK], big_m.shape)
                b_last = jnp.broadcast_to(b[:, CHUNK - 1:CHUNK], b.shape)
                yield
                inter = jnp.exp(m_run - big_m)
                efloor = jnp.exp(-(b + big_m))
                grow_w[c, G_A:G_A + SUBLANES, :] = a
                grow_w[c, G_W:G_W + SUBLANES, :] = jnp.exp(a - m_last)
                grow_w[c, G_DECAY:G_DECAY + SUBLANES, :] = jnp.exp(m_run - m_last)
                gcol_w[c] = jnp.concatenate([big_m, inter, efloor, zero_rows], axis=0).T
                m_run = b_last + m_last
            m_ref[...] = m_run

        def pool_chunk(c, slot):
            r0 = c * CHUNK
            for g, w in enumerate(POOL_WINDOWS):
                cs = slice(g * POOL_GROUP_DIM, (g + 1) * POOL_GROUP_DIM)
                pg = jnp.dot(band_ref[g], u_r[r0:r0 + 2 * CHUNK, cs],
                             preferred_element_type=F32)
                if c == 0:
                    cur = u_r[POOL_HIST:POOL_HIST + POOL_FIX_ROWS, cs].astype(F32)
                    pos = lax.broadcasted_iota(jnp.int32, (POOL_FIX_ROWS, POOL_GROUP_DIM), 0) + 1
                    short = float(w) / jnp.minimum(pos, w).astype(F32) - 1.0
                    scale = 1.0 + first_of_seq.astype(F32) * short
                    fixed = (pg[0:POOL_FIX_ROWS] + cur) * scale - cur
                    pooled_ref[slot, 0:POOL_FIX_ROWS, cs] = fixed.astype(BF16)
                    pooled_ref[slot, POOL_FIX_ROWS:, cs] = pg[POOL_FIX_ROWS:].astype(BF16)
                else:
                    pooled_ref[slot, :, cs] = pg.astype(BF16)
            yield
            for pair in range(len(POOL_WINDOWS) // 2):
                cs = slice(pair * MXU_COLS, (pair + 1) * MXU_COLS)
                y = jnp.dot(pooled_ref[slot, :, cs], wp_ref[pair], preferred_element_type=F32)
                for rs in _strips(CHUNK, STRIP):
                    rows = slice(r0 + rs.start, r0 + rs.stop)
                    zp = rest_r[rows, R_ZP + pair * MXU_COLS:R_ZP + (pair + 1) * MXU_COLS]
                    mix_ref[rows, cs] = (y[rs] * ls_ref[:, cs] * _silu(zp)).astype(BF16)

        def head_chunk(c, hd, slot):
            r0 = c * CHUNK
            hs = slice(hd * HEAD_DIM, (hd + 1) * HEAD_DIM)
            ks = slice(D_MLSTM + hd * HEAD_DIM, D_MLSTM + (hd + 1) * HEAD_DIM)

            def conv(cols_, rs):
                n = STRIP // SUBLANES
                base = CONV_HIST + r0 + rs.start - SUBLANES
                vs = [qk_r[base + SUBLANES * i:base + SUBLANES * (i + 1), cols_] for i in range(n + 1)]
                row = lax.broadcasted_iota(jnp.int32, (SUBLANES, HEAD_DIM), 0)
                w_now = cw_ref[CONV_WIDTH - 1:CONV_WIDTH, cols_]
                ys = [cb_ref[:, cols_] + w_now * vs[i + 1] for i in range(n)]
                for k in range(1, CONV_WIDTH):
                    w_k = cw_ref[CONV_WIDTH - 1 - k:CONV_WIDTH - k, cols_]
                    rolled = [pltpu.roll(v, k, axis=0) for v in vs]
                    ys = [ys[i] + w_k * jnp.where(row >= k, rolled[i + 1], rolled[i]) for i in range(n)]
                return _silu(jnp.concatenate(ys, axis=0))

            def gcol(col, rs):
                return gcol_r[c, rs, col + hd:col + hd + 1]

            for rs in _strips(CHUNK, STRIP):
                kf_ref[slot, rs, :] = conv(ks, rs) * (HEAD_DIM ** -0.5)
            yield
            kt = kf_ref[slot].T
            w_row = grow_r[c, G_W + hd:G_W + hd + 1, :]
            for rs in _strips(HEAD_DIM, STRIP):
                ktbf_ref[slot, rs, :] = kt[rs].astype(BF16)
                kwt_ref[slot, rs, :] = (kt[rs] * w_row).astype(BF16)
                rhs2_ref[slot, CHUNK + rs.start:CHUNK + rs.stop, :] = caug_ref[hd, rs, :].astype(BF16)
            yield
            for rs in _strips(CHUNK, STRIP):
                rows = slice(r0 + rs.start, r0 + rs.stop)
                q = conv(hs, rs)
                qbf_ref[slot, rs, :] = q.astype(BF16)
                lhs2_ref[slot, rs, CHUNK:] = (q * gcol(GC_INTER, rs)).astype(BF16)
                v = rest_r[rows, R_V + hd * HEAD_DIM:R_V + (hd + 1) * HEAD_DIM]
                rhs2_ref[slot, rs, 0:HEAD_DIM] = v.astype(BF16)
            yield
            sc = jnp.dot(qbf_ref[slot], ktbf_ref[slot], preferred_element_type=F32)
            upd = jnp.dot(kwt_ref[slot], rhs2_ref[slot, 0:CHUNK, :], preferred_element_type=F32)
            dec = grow_r[c, G_DECAY + hd:G_DECAY + hd + 1, :]
            dec2 = jnp.concatenate([dec, dec], axis=1)
            for rs in _strips(HEAD_DIM, STRIP):
                caug_ref[hd, rs, :] = dec2 * caug_ref[hd, rs, :] + upd[rs]
            yield
            a_row = grow_r[c, G_A + hd:G_A + hd + 1, :]
            for rs in _strips(CHUNK, STRIP):
                row_i = lax.broadcasted_iota(jnp.int32, (STRIP, CHUNK), 0) + rs.start
                col_i = lax.broadcasted_iota(jnp.int32, (STRIP, CHUNK), 1)
                dmat = jnp.where(col_i <= row_i, jnp.exp(a_row - gcol(GC_M, rs)), 0.0)
                lhs2_ref[slot, rs, 0:CHUNK] = (sc[rs] * dmat).astype(BF16)
            yield
            numden = jnp.dot(lhs2_ref[slot], rhs2_ref[slot], preferred_element_type=F32)
            yield
            for rs in _strips(CHUNK, STRIP):
                rows = slice(r0 + rs.start, r0 + rs.stop)
                num = numden[rs, :HEAD_DIM]
                den = numden[rs, HEAD_DIM:]
                hm = num / jnp.maximum(jnp.abs(den), gcol(GC_EFLOOR, rs))
                ms = jnp.mean(hm * hm, axis=-1, keepdims=True)
                og = jax.nn.sigmoid(rest_r[rows, R_O + hd * HEAD_DIM:R_O + (hd + 1) * HEAD_DIM])
                zm = rest_r[rows, R_ZM + hd * HEAD_DIM:R_ZM + (hd + 1) * HEAD_DIM]
                hm = hm * lax.rsqrt(ms + EPS) * mg_ref[:, hs]
                mix_ref[rows, D_POOL + hd * HEAD_DIM:D_POOL + (hd + 1) * HEAD_DIM] = (
                    hm * og * _silu(zm)).astype(BF16)

        def out_block(rb):
            rows = slice(rb * OUT_ROWS, (rb + 1) * OUT_ROWS)
            y = jnp.dot(mix_ref[rows, :], wout_ref[...], preferred_element_type=F32)
            o_ref[rows, :] = xres_ref[rows, :] + gate * y
            for rs in _strips(OUT_ROWS, NORM_STRIP):
                rows = slice(rb * OUT_ROWS + rs.start, rb * OUT_ROWS + rs.stop)
                z = o_ref[rows, :]
                ms = jnp.mean(z * z, axis=-1, keepdims=True)
                o_ref[rows, :] = z * lax.rsqrt(ms + EPS) * fg_ref[...]

        proj_fill = [lambda p=p, rb=rb: proj_piece(p, rb)
                     for p in _proj_pieces() for rb in range(T // PROJ_ROWS)]
        pools = [pool_chunk(c, c) for c in range(N_CHUNKS)]
        pool_fill = [lambda c=c: next(pools[c], None) for c in POOL_ORDER]
        fill = _merge_evenly([proj_fill, pool_fill])
        norm_fill = [lambda i=i: norm_block(i) for i in range(T // NORM_ROWS)]

        def emit(work, n):
            for _ in range(n):
                if work:
                    work.pop(0)()

        def advance(gen):
            try:
                next(gen)
                return True
            except StopIteration:
                return False

        units = [head_chunk(c, hd, c * N_HEADS + hd)
                 for c in range(N_CHUNKS) for hd in range(N_HEADS)]
        n_rounds = ROUNDS_PER_TILE
        len_fill = len(fill)
        gates = gates_tile()
        advance(gates)
        active = []
        rounds = 0
        while units or active:
            while units and len(active) < IN_FLIGHT:
                active.append(units.pop(0))
            active = [gen for gen in active if advance(gen)]
            rounds += 1
            emit(fill, (rounds * len_fill) // n_rounds - ((rounds - 1) * len_fill) // n_rounds)
            advance(gates)
        for _ in gates:
            pass
        emit(fill, len(fill))

        u_w[0:POOL_HIST, :] = u_r[T:T + POOL_HIST, :]
        qk_w[0:CONV_HIST, :] = qk_r[T:T + CONV_HIST, :]

        for rb in range(T // OUT_ROWS):
            emit(norm_fill, len(norm_fill) // (T // OUT_ROWS))
            out_block(rb)
        emit(norm_fill, len(norm_fill))

    return block_kernel


def _const_spec(shape):
    return pl.BlockSpec(shape, lambda s: (0,) * len(shape), pipeline_mode=pl.Buffered(1))


@jax.jit
def kernel(x, c, norm_g, w_ada, b_ada, w_in, b_gates, conv_w, conv_b, w_pool, ls_pool,
           mh_norm_g, w_out, final_g):
    B, S, D = x.shape
    T = SEQ_TILE
    assert D == D_MODEL and S % T == 0
    assert w_in.shape == (1, D_MODEL, OFF_G + 2 * N_HEADS)
    tiles_per_seq = S // T
    n_tiles = B * tiles_per_seq

    n_ada = 3
    mod = pl.pallas_call(
        _ada_kernel,
        grid=(n_ada,),
        in_specs=[pl.BlockSpec((B, D), lambda n: (0, 0)),
                  pl.BlockSpec((None, D, D), lambda n: (0, 0, n)),
                  pl.BlockSpec((1, D), lambda n: (0, n))],
        out_specs=pl.BlockSpec((B, D), lambda n: (0, n)),
        out_shape=jax.ShapeDtypeStruct((B, n_ada * D), F32),
    )(c, w_ada, b_ada)
    mod = mod.reshape(B, n_ada, D)

    pad = GATE_PAD - 2 * N_HEADS
    w_main = w_in[0].astype(BF16)
    w_gates = jnp.pad(w_in[0][:, OFF_G:], ((0, 0), (0, pad))).astype(BF16)
    b_gates_p = jnp.pad(b_gates, ((0, 0), (0, pad)))
    n_pairs = len(POOL_WINDOWS) // 2
    wp = w_pool[0].astype(BF16).reshape(n_pairs, 2, POOL_GROUP_DIM, POOL_GROUP_DIM)
    zeros = jnp.zeros((n_pairs, POOL_GROUP_DIM, POOL_GROUP_DIM), BF16)
    wp_bd = jnp.concatenate([jnp.concatenate([wp[:, 0], zeros], axis=2),
                             jnp.concatenate([zeros, wp[:, 1]], axis=2)], axis=1)
    x_tiles = x.reshape(n_tiles, T, D)

    def tile_n(s):
        return (jnp.minimum(s, n_tiles - 1), 0, 0)

    def tile_c(s):
        return (jnp.clip(s - (PIPE_DEPTH - 1), 0, n_tiles - 1), 0, 0)

    out = pl.pallas_call(
        _make_block_kernel(tiles_per_seq, n_tiles),
        grid=(n_tiles + PIPE_DEPTH - 1,),
        in_specs=[
            pl.BlockSpec((None, T, D), tile_n),
            pl.BlockSpec((None, T, D), tile_c),
            _const_spec((B, n_ada, D)),
            _const_spec((1, D)),
            _const_spec((D, OFF_G + 2 * N_HEADS)),
            _const_spec((D, GATE_PAD)),
            _const_spec((1, GATE_PAD)),
            _const_spec((CONV_WIDTH, 2 * D_MLSTM)),
            _const_spec((1, 2 * D_MLSTM)),
            _const_spec((n_pairs, MXU_COLS, MXU_COLS)),
            _const_spec((1, D_POOL)),
            _const_spec((1, D_MLSTM)),
            _const_spec((D, D)),
            _const_spec((1, D)),
        ],
        out_specs=pl.BlockSpec((None, T, D), tile_c),
        out_shape=jax.ShapeDtypeStruct((n_tiles, T, D), x.dtype),
        scratch_shapes=[
            *2 * [pltpu.VMEM((T, D), BF16)],
            *2 * [pltpu.VMEM((POOL_HIST + T, D_POOL), BF16)],
            *2 * [pltpu.VMEM((CONV_HIST + T, 2 * D_MLSTM), F32)],
            *2 * [pltpu.VMEM((T, R_COLS), F32)],
            *2 * [pltpu.VMEM((N_CHUNKS, CHUNK, CHUNK), F32)],
            *2 * [pltpu.VMEM((N_CHUNKS, G_ROWS, CHUNK), F32)],
            pltpu.VMEM((len(POOL_WINDOWS), CHUNK, 2 * CHUNK), BF16),
            pltpu.VMEM((T, D), BF16),
            pltpu.VMEM((N_HEADS, HEAD_DIM, 2 * HEAD_DIM), F32),
            pltpu.VMEM((SUBLANES, CHUNK), F32),
            pltpu.VMEM((N_CHUNKS, CHUNK, D_POOL), BF16),
            pltpu.VMEM((N_HEAD_UNITS, CHUNK, HEAD_DIM), F32),
            pltpu.VMEM((N_HEAD_UNITS, HEAD_DIM, CHUNK), BF16),
            pltpu.VMEM((N_HEAD_UNITS, HEAD_DIM, CHUNK), BF16),
            pltpu.VMEM((N_HEAD_UNITS, CHUNK, HEAD_DIM), BF16),
            pltpu.VMEM((N_HEAD_UNITS, CHUNK, CHUNK + HEAD_DIM), BF16),
            pltpu.VMEM((N_HEAD_UNITS, CHUNK + HEAD_DIM, 2 * HEAD_DIM), BF16),
        ],
        compiler_params=pltpu.CompilerParams(
            dimension_semantics=("arbitrary",),
            vmem_limit_bytes=VMEM_LIMIT),
    )(x_tiles, x_tiles, mod, norm_g, w_main, w_gates, b_gates_p, conv_w[0], conv_b,
      wp_bd, ls_pool, mh_norm_g, w_out[0].astype(BF16), final_g.reshape(1, D))
    return out.reshape(B, S, D)
```

```python
import jax
import jax.numpy as jnp
from jax import lax
from jax.experimental import pallas as pl
from jax.experimental.pallas import tpu as pltpu

F32 = jnp.float32
BF16 = jnp.bfloat16

D_MODEL = 1024
D_POOL = 512
D_MLSTM = 512
POOL_WINDOWS = (2, 4, 8, 16)
POOL_GROUP_DIM = 128
N_HEADS = 4
HEAD_DIM = 128
CONV_WIDTH = 4
CHUNK = 128
EPS = 1e-6

LANES = 128
SUBLANES = 8
MXU_COLS = 256
POOL_HIST = CHUNK
POOL_FIX_ROWS = 16
CONV_HIST = SUBLANES
GATE_PAD = LANES

OFF_U = 0
OFF_ZP = 512
OFF_QK = 1024
OFF_V = 2048
OFF_G = 3584

R_ZP = 0
R_V = 512
R_O = 1024
R_ZM = 1536
R_COLS = 2048

G_A = 0
G_W = SUBLANES
G_DECAY = 2 * SUBLANES
G_ROWS = 3 * SUBLANES

GC_M = 0
GC_INTER = SUBLANES
GC_EFLOOR = 2 * SUBLANES

SEQ_TILE = 512
N_CHUNKS = SEQ_TILE // CHUNK
STRIP = 128
NORM_STRIP = 16
NORM_ROWS = 64
OUT_ROWS = 512
PROJ_ROWS = 256
PROJ_COLS = 2 * MXU_COLS
PIPE_DEPTH = 3
N_PAIRED = 6
N_HEAD_UNITS = N_CHUNKS * N_HEADS
IN_FLIGHT = 3
ROUNDS_PER_TILE = 44
POOL_ORDER = (0, 1, 0, 2, 1, 3, 2, 3)
VMEM_LIMIT = 56 * 1024 * 1024


def _sigmoid(z):
    return 0.5 * jnp.tanh(0.5 * z) + 0.5


def _silu(z):
    hz = 0.5 * z
    return hz * jnp.tanh(hz) + hz


def _log_sigmoid(z):
    return jnp.minimum(z, 0.0) - jnp.log1p(jnp.exp(-jnp.abs(z)))


def _strips(n, step):
    return [slice(r, r + step) for r in range(0, n, step)]


def _merge_evenly(lists):
    merged = []
    total = sum(len(l) for l in lists)
    taken = [0] * len(lists)
    for i in range(1, total + 1):
        for j, l in enumerate(lists):
            while taken[j] < (i * len(l)) // total:
                merged.append(l[taken[j]])
                taken[j] += 1
    return merged


def _ada_kernel(c_ref, w_ref, b_ref, o_ref):
    c = c_ref[...]
    o_ref[...] = jnp.dot(_silu(c), w_ref[...], preferred_element_type=F32) + b_ref[...]


def _proj_pieces():
    pieces = []
    for lo in range(OFF_U, OFF_ZP, PROJ_COLS):
        pieces.append((lo, PROJ_COLS, "u", lo - OFF_U))
    for lo in range(OFF_ZP, OFF_QK, PROJ_COLS):
        pieces.append((lo, PROJ_COLS, "rest", R_ZP + lo - OFF_ZP))
    for lo in range(OFF_QK, OFF_V, PROJ_COLS):
        pieces.append((lo, PROJ_COLS, "qk", lo - OFF_QK))
    for lo in range(OFF_V, OFF_G, PROJ_COLS):
        pieces.append((lo, PROJ_COLS, "rest", R_V + lo - OFF_V))
    return pieces


def _make_block_kernel(tiles_per_seq, n_tiles):
    T = SEQ_TILE

    def block_kernel(x_ref, xres_ref, mod_ref, ng_ref, win_ref, wg_ref, bg_ref, cw_ref, cb_ref, wp_ref,
                     ls_ref, mg_ref, wout_ref, fg_ref, o_ref, *scratch):
        s = pl.program_id(0)
        pairs, shared = scratch[:2 * N_PAIRED], scratch[2 * N_PAIRED:]
        band_ref, caug_ref, m_ref, rhs2_ref = shared[0], shared[2], shared[3], shared[-1]

        @pl.when(s == 0)
        def _():
            for ref in pairs + (caug_ref, m_ref):
                ref[...] = jnp.zeros(ref.shape, ref.dtype)
            rhs2_ref[:, 0:CHUNK, HEAD_DIM:] = jnp.ones((N_HEAD_UNITS, CHUNK, HEAD_DIM), BF16)
            t_i = lax.broadcasted_iota(jnp.int32, (CHUNK, 2 * CHUNK), 0)
            r_i = lax.broadcasted_iota(jnp.int32, (CHUNK, 2 * CHUNK), 1) - POOL_HIST
            for g, w in enumerate(POOL_WINDOWS):
                in_window = (r_i <= t_i) & (r_i > t_i - w)
                band = jnp.where(in_window, 1.0 / w, 0.0) - jnp.where(r_i == t_i, 1.0, 0.0)
                band_ref[g] = band.astype(BF16)

        inputs = (x_ref, xres_ref, mod_ref, ng_ref, win_ref, wg_ref, bg_ref, cw_ref, cb_ref, wp_ref,
                  ls_ref, mg_ref, wout_ref, fg_ref, o_ref)
        for parity in range(2):
            written = pairs[parity::2]
            read = pairs[1 - parity::2]
            bufs = tuple(r for wr in zip(written, read) for r in wr)

            @pl.when(s % 2 == parity)
            def _(bufs=bufs):
                step(s, *inputs, *bufs, *shared)

    def step(s, x_ref, xres_ref, mod_ref, ng_ref, win_ref, wg_ref, bg_ref, cw_ref, cb_ref, wp_ref,
             ls_ref, mg_ref, wout_ref, fg_ref, o_ref,
             h_w, h_r, u_w, u_r, qk_w, qk_r, rest_w, rest_r, gcol_w, gcol_r, grow_w, grow_r,
             band_ref, mix_ref, caug_ref, m_ref, pooled_ref, kf_ref, ktbf_ref, kwt_ref, qbf_ref,
             lhs2_ref, rhs2_ref):
        tile_n = jnp.minimum(s, n_tiles - 1)
        tile_c = jnp.clip(s - (PIPE_DEPTH - 1), 0, n_tiles - 1)

        first_of_seq = (s - (PIPE_DEPTH - 1)) % tiles_per_seq == 0
        first_of_seq_g = (s - 1) % tiles_per_seq == 0

        @pl.when(first_of_seq)
        def _():
            u_r[0:POOL_HIST, :] = jnp.zeros((POOL_HIST, D_POOL), BF16)
            qk_r[0:CONV_HIST, :] = jnp.zeros((CONV_HIST, 2 * D_MLSTM), F32)
            caug_ref[...] = jnp.zeros(caug_ref.shape, F32)

        mod_n = mod_ref[tile_n // tiles_per_seq]
        shift = mod_n[0:1, :]
        gscale = ng_ref[...] * (1.0 + mod_n[1:2, :])
        gate = mod_ref[tile_c // tiles_per_seq][2:3, :]

        def norm_block(i):
            for rs in _strips(NORM_ROWS, NORM_STRIP):
                rows = slice(i * NORM_ROWS + rs.start, i * NORM_ROWS + rs.stop)
                x = x_ref[rows, :]
                ms = jnp.mean(x * x, axis=-1, keepdims=True)
                h_w[rows, :] = (x * lax.rsqrt(ms + EPS) * gscale + shift).astype(BF16)

        def proj_piece(piece, rb):
            lo, width, dst, col = piece
            r0, r1 = rb * PROJ_ROWS, (rb + 1) * PROJ_ROWS
            y = jnp.dot(h_r[r0:r1, :], win_ref[:, lo:lo + width],
                        preferred_element_type=F32)
            if dst == "u":
                u_w[POOL_HIST + r0:POOL_HIST + r1, col:col + width] = y.astype(BF16)
            elif dst == "qk":
                qk_w[CONV_HIST + r0:CONV_HIST + r1, col:col + width] = y
            else:
                rest_w[r0:r1, col:col + width] = y

        zero_rows = jnp.zeros((CHUNK - 3 * SUBLANES, CHUNK), F32)

        def gates_tile():
            gts = []
            for rb in range(T // PROJ_ROWS):
                g_pre = jnp.dot(h_r[rb * PROJ_ROWS:(rb + 1) * PROJ_ROWS, :],
                                wg_ref[...],
                                preferred_element_type=F32) + bg_ref[...]
                for cc in range(PROJ_ROWS // CHUNK):
                    gts.append(g_pre[cc * CHUNK:(cc + 1) * CHUNK, :].T[0:SUBLANES, :])
            yield
            lane = lax.broadcasted_iota(jnp.int32, (SUBLANES, CHUNK), 1)
            bs = [pltpu.roll(_log_sigmoid(gt), N_HEADS, axis=0) for gt in gts]
            k = 1
            while k < CHUNK:
                bs = [x + jnp.where(lane >= k, pltpu.roll(x, k, axis=1), 0.0) for x in bs]
                k *= 2
                yield
            a_s = [gt - b for gt, b in zip(gts, bs)]
            cm = a_s
            k = 1
            while k < CHUNK:
                cm = [jnp.maximum(x, jnp.where(lane >= k, pltpu.roll(x, k, axis=1), -jnp.inf))
                      for x in cm]
                k *= 2
                yield
            m_run = jnp.where(first_of_seq_g, 0.0, m_ref[...])
            for c in range(N_CHUNKS):
                a, b = a_s[c], bs[c]
                big_m = jnp.maximum(m_run, cm[c])
                m_last = jnp.broadcast_to(big_m[:, CHUNK - 1:CHUNK], big_m.shape)
                b_last = jnp.broadcast_to(b[:, CHUNK - 1:CHUNK], b.shape)
                yield
                inter = jnp.exp(m_run - big_m)
                efloor = jnp.exp(-(b + big_m))
                grow_w[c, G_A:G_A + SUBLANES, :] = a
                grow_w[c, G_W:G_W + SUBLANES, :] = jnp.exp(a - m_last)
                grow_w[c, G_DECAY:G_DECAY + SUBLANES, :] = jnp.exp(m_run - m_last)
                gcol_w[c] = jnp.concatenate([big_m, inter, efloor, zero_rows], axis=0).T
                m_run = b_last + m_last
            m_ref[...] = m_run

        def pool_chunk(c, slot):
            r0 = c * CHUNK
            for g, w in enumerate(POOL_WINDOWS):
                cs = slice(g * POOL_GROUP_DIM, (g + 1) * POOL_GROUP_DIM)
                pg = jnp.dot(band_ref[g], u_r[r0:r0 + 2 * CHUNK, cs],
                             preferred_element_type=F32)
                if c == 0:
                    cur = u_r[POOL_HIST:POOL_HIST + POOL_FIX_ROWS, cs].astype(F32)
                    pos = lax.broadcasted_iota(jnp.int32, (POOL_FIX_ROWS, POOL_GROUP_DIM), 0) + 1
                    short = float(w) / jnp.minimum(pos, w).astype(F32) - 1.0
                    scale = 1.0 + first_of_seq.astype(F32) * short
                    fixed = (pg[0:POOL_FIX_ROWS] + cur) * scale - cur
                    pooled_ref[slot, 0:POOL_FIX_ROWS, cs] = fixed.astype(BF16)
                    pooled_ref[slot, POOL_FIX_ROWS:, cs] = pg[POOL_FIX_ROWS:].astype(BF16)
                else:
                    pooled_ref[slot, :, cs] = pg.astype(BF16)
            yield
            for pair in range(len(POOL_WINDOWS) // 2):
                cs = slice(pair * MXU_COLS, (pair + 1) * MXU_COLS)
                y = jnp.dot(pooled_ref[slot, :, cs], wp_ref[pair], preferred_element_type=F32)
                for rs in _strips(CHUNK, STRIP):
                    rows = slice(r0 + rs.start, r0 + rs.stop)
                    zp = rest_r[rows, R_ZP + pair * MXU_COLS:R_ZP + (pair + 1) * MXU_COLS]
                    mix_ref[rows, cs] = (y[rs] * ls_ref[:, cs] * _silu(zp)).astype(BF16)

        def head_chunk(c, hd, slot):
            r0 = c * CHUNK
            hs = slice(hd * HEAD_DIM, (hd + 1) * HEAD_DIM)
            ks = slice(D_MLSTM + hd * HEAD_DIM, D_MLSTM + (hd + 1) * HEAD_DIM)

            def conv(cols_, rs):
                n = STRIP // SUBLANES
                base = CONV_HIST + r0 + rs.start - SUBLANES
                vs = [qk_r[base + SUBLANES * i:base + SUBLANES * (i + 1), cols_] for i in range(n + 1)]
                row = lax.broadcasted_iota(jnp.int32, (SUBLANES, HEAD_DIM), 0)
                w_now = cw_ref[CONV_WIDTH - 1:CONV_WIDTH, cols_]
                ys = [cb_ref[:, cols_] + w_now * vs[i + 1] for i in range(n)]
                for k in range(1, CONV_WIDTH):
                    w_k = cw_ref[CONV_WIDTH - 1 - k:CONV_WIDTH - k, cols_]
                    rolled = [pltpu.roll(v, k, axis=0) for v in vs]
                    ys = [ys[i] + w_k * jnp.where(row >= k, rolled[i + 1], rolled[i]) for i in range(n)]
                return _silu(jnp.concatenate(ys, axis=0))

            def gcol(col, rs):
                return gcol_r[c, rs, col + hd:col + hd + 1]

            for rs in _strips(CHUNK, STRIP):
                kf_ref[slot, rs, :] = conv(ks, rs) * (HEAD_DIM ** -0.5)
            yield
            kt = kf_ref[slot].T
            w_row = grow_r[c, G_W + hd:G_W + hd + 1, :]
            for rs in _strips(HEAD_DIM, STRIP):
                ktbf_ref[slot, rs, :] = kt[rs].astype(BF16)
                kwt_ref[slot, rs, :] = (kt[rs] * w_row).astype(BF16)
                rhs2_ref[slot, CHUNK + rs.start:CHUNK + rs.stop, :] = caug_ref[hd, rs, :].astype(BF16)
            yield
            for rs in _strips(CHUNK, STRIP):
                rows = slice(r0 + rs.start, r0 + rs.stop)
                q = conv(hs, rs)
                qbf_ref[slot, rs, :] = q.astype(BF16)
                lhs2_ref[slot, rs, CHUNK:] = (q * gcol(GC_INTER, rs)).astype(BF16)
                v = rest_r[rows, R_V + hd * HEAD_DIM:R_V + (hd + 1) * HEAD_DIM]
                rhs2_ref[slot, rs, 0:HEAD_DIM] = v.astype(BF16)
            yield
            sc = jnp.dot(qbf_ref[slot], ktbf_ref[slot], preferred_element_type=F32)
            upd = jnp.dot(kwt_ref[slot], rhs2_ref[slot, 0:CHUNK, :], preferred_element_type=F32)
            dec = grow_r[c, G_DECAY + hd:G_DECAY + hd + 1, :]
            dec2 = jnp.concatenate([dec, dec], axis=1)
            for rs in _strips(HEAD_DIM, STRIP):
                caug_ref[hd, rs, :] = dec2 * caug_ref[hd, rs, :] + upd[rs]
            yield
            a_row = grow_r[c, G_A + hd:G_A + hd + 1, :]
            for rs in _strips(CHUNK, STRIP):
                row_i = lax.broadcasted_iota(jnp.int32, (STRIP, CHUNK), 0) + rs.start
                col_i = lax.broadcasted_iota(jnp.int32, (STRIP, CHUNK), 1)
                dmat = jnp.where(col_i <= row_i, jnp.exp(a_row - gcol(GC_M, rs)), 0.0)
                lhs2_ref[slot, rs, 0:CHUNK] = (sc[rs] * dmat).astype(BF16)
            yield
            numden = jnp.dot(lhs2_ref[slot], rhs2_ref[slot], preferred_element_type=F32)
            yield
            for rs in _strips(CHUNK, STRIP):
                rows = slice(r0 + rs.start, r0 + rs.stop)
                num = numden[rs, :HEAD_DIM]
                den = numden[rs, HEAD_DIM:]
                hm = num / jnp.maximum(jnp.abs(den), gcol(GC_EFLOOR, rs))
                ms = jnp.mean(hm * hm, axis=-1, keepdims=True)
                og = _sigmoid(rest_r[rows, R_O + hd * HEAD_DIM:R_O + (hd + 1) * HEAD_DIM])
                zm = rest_r[rows, R_ZM + hd * HEAD_DIM:R_ZM + (hd + 1) * HEAD_DIM]
                hm = hm * lax.rsqrt(ms + EPS) * mg_ref[:, hs]
                mix_ref[rows, D_POOL + hd * HEAD_DIM:D_POOL + (hd + 1) * HEAD_DIM] = (
                    hm * og * _silu(zm)).astype(BF16)

        def out_block(rb):
            rows = slice(rb * OUT_ROWS, (rb + 1) * OUT_ROWS)
            y = jnp.dot(mix_ref[rows, :], wout_ref[...], preferred_element_type=F32)
            o_ref[rows, :] = xres_ref[rows, :] + gate * y
            for rs in _strips(OUT_ROWS, NORM_STRIP):
                rows = slice(rb * OUT_ROWS + rs.start, rb * OUT_ROWS + rs.stop)
                z = o_ref[rows, :]
                ms = jnp.mean(z * z, axis=-1, keepdims=True)
                o_ref[rows, :] = z * lax.rsqrt(ms + EPS) * fg_ref[...]

        proj_fill = [lambda p=p, rb=rb: proj_piece(p, rb)
                     for p in _proj_pieces() for rb in range(T // PROJ_ROWS)]
        pools = [pool_chunk(c, c) for c in range(N_CHUNKS)]
        pool_fill = [lambda c=c: next(pools[c], None) for c in POOL_ORDER]
        fill = _merge_evenly([proj_fill, pool_fill])
        norm_fill = [lambda i=i: norm_block(i) for i in range(T // NORM_ROWS)]

        def emit(work, n):
            for _ in range(n):
                if work:
                    work.pop(0)()

        def advance(gen):
            try:
                next(gen)
                return True
            except StopIteration:
                return False

        units = [head_chunk(c, hd, c * N_HEADS + hd)
                 for c in range(N_CHUNKS) for hd in range(N_HEADS)]
        n_rounds = ROUNDS_PER_TILE
        len_fill = len(fill)
        gates = gates_tile()
        advance(gates)
        active = []
        rounds = 0
        while units or active:
            while units and len(active) < IN_FLIGHT:
                active.append(units.pop(0))
            active = [gen for gen in active if advance(gen)]
            rounds += 1
            emit(fill, (rounds * len_fill) // n_rounds - ((rounds - 1) * len_fill) // n_rounds)
            advance(gates)
        for _ in gates:
            pass
        emit(fill, len(fill))

        u_w[0:POOL_HIST, :] = u_r[T:T + POOL_HIST, :]
        qk_w[0:CONV_HIST, :] = qk_r[T:T + CONV_HIST, :]

        for rb in range(T // OUT_ROWS):
            emit(norm_fill, len(norm_fill) // (T // OUT_ROWS))
            out_block(rb)
        emit(norm_fill, len(norm_fill))

    return block_kernel


def _const_spec(shape):
    return pl.BlockSpec(shape, lambda s: (0,) * len(shape), pipeline_mode=pl.Buffered(1))


@jax.jit
def kernel(x, c, norm_g, w_ada, b_ada, w_in, b_gates, conv_w, conv_b, w_pool, ls_pool,
           mh_norm_g, w_out, final_g):
    B, S, D = x.shape
    T = SEQ_TILE
    assert D == D_MODEL and S % T == 0
    assert w_in.shape == (1, D_MODEL, OFF_G + 2 * N_HEADS)
    tiles_per_seq = S // T
    n_tiles = B * tiles_per_seq

    n_ada = 3
    mod = pl.pallas_call(
        _ada_kernel,
        grid=(n_ada,),
        in_specs=[pl.BlockSpec((B, D), lambda n: (0, 0)),
                  pl.BlockSpec((None, D, D), lambda n: (0, 0, n)),
                  pl.BlockSpec((1, D), lambda n: (0, n))],
        out_specs=pl.BlockSpec((B, D), lambda n: (0, n)),
        out_shape=jax.ShapeDtypeStruct((B, n_ada * D), F32),
    )(c, w_ada, b_ada)
    mod = mod.reshape(B, n_ada, D)

    pad = GATE_PAD - 2 * N_HEADS
    w_main = w_in[0].astype(BF16)
    w_gates = jnp.pad(w_in[0][:, OFF_G:], ((0, 0), (0, pad))).astype(BF16)
    b_gates_p = jnp.pad(b_gates, ((0, 0), (0, pad)))
    n_pairs = len(POOL_WINDOWS) // 2
    wp = w_pool[0].astype(BF16).reshape(n_pairs, 2, POOL_GROUP_DIM, POOL_GROUP_DIM)
    zeros = jnp.zeros((n_pairs, POOL_GROUP_DIM, POOL_GROUP_DIM), BF16)
    wp_bd = jnp.concatenate([jnp.concatenate([wp[:, 0], zeros], axis=2),
                             jnp.concatenate([zeros, wp[:, 1]], axis=2)], axis=1)
    x_tiles = x.reshape(n_tiles, T, D)

    def tile_n(s):
        return (jnp.minimum(s, n_tiles - 1), 0, 0)

    def tile_c(s):
        return (jnp.clip(s - (PIPE_DEPTH - 1), 0, n_tiles - 1), 0, 0)

    out = pl.pallas_call(
        _make_block_kernel(tiles_per_seq, n_tiles),
        grid=(n_tiles + PIPE_DEPTH - 1,),
        in_specs=[
            pl.BlockSpec((None, T, D), tile_n),
            pl.BlockSpec((None, T, D), tile_c),
            _const_spec((B, n_ada, D)),
            _const_spec((1, D)),
            _const_spec((D, OFF_G + 2 * N_HEADS)),
            _const_spec((D, GATE_PAD)),
            _const_spec((1, GATE_PAD)),
            _const_spec((CONV_WIDTH, 2 * D_MLSTM)),
            _const_spec((1, 2 * D_MLSTM)),
            _const_spec((n_pairs, MXU_COLS, MXU_COLS)),
            _const_spec((1, D_POOL)),
            _const_spec((1, D_MLSTM)),
            _const_spec((D, D)),
            _const_spec((1, D)),
        ],
        out_specs=pl.BlockSpec((None, T, D), tile_c),
        out_shape=jax.ShapeDtypeStruct((n_tiles, T, D), x.dtype),
        scratch_shapes=[
            *2 * [pltpu.VMEM((T, D), BF16)],
            *2 * [pltpu.VMEM((POOL_HIST + T, D_POOL), BF16)],
            *2 * [pltpu.VMEM((CONV_HIST + T, 2 * D_MLSTM), F32)],
            *2 * [pltpu.VMEM((T, R_COLS), F32)],
            *2 * [pltpu.VMEM((N_CHUNKS, CHUNK, CHUNK), F32)],
            *2 * [pltpu.VMEM((N_CHUNKS, G_ROWS, CHUNK), F32)],
            pltpu.VMEM((len(POOL_WINDOWS), CHUNK, 2 * CHUNK), BF16),
            pltpu.VMEM((T, D), BF16),
            pltpu.VMEM((N_HEADS, HEAD_DIM, 2 * HEAD_DIM), F32),
            pltpu.VMEM((SUBLANES, CHUNK), F32),
            pltpu.VMEM((N_CHUNKS, CHUNK, D_POOL), BF16),
            pltpu.VMEM((N_HEAD_UNITS, CHUNK, HEAD_DIM), F32),
            pltpu.VMEM((N_HEAD_UNITS, HEAD_DIM, CHUNK), BF16),
            pltpu.VMEM((N_HEAD_UNITS, HEAD_DIM, CHUNK), BF16),
            pltpu.VMEM((N_HEAD_UNITS, CHUNK, HEAD_DIM), BF16),
            pltpu.VMEM((N_HEAD_UNITS, CHUNK, CHUNK + HEAD_DIM), BF16),
            pltpu.VMEM((N_HEAD_UNITS, CHUNK + HEAD_DIM, 2 * HEAD_DIM), BF16),
        ],
        compiler_params=pltpu.CompilerParams(
            dimension_semantics=("arbitrary",),
            vmem_limit_bytes=VMEM_LIMIT),
    )(x_tiles, x_tiles, mod, norm_g, w_main, w_gates, b_gates_p, conv_w[0], conv_b,
      wp_bd, ls_pool, mh_norm_g, w_out[0].astype(BF16), final_g.reshape(1, D))
    return out.reshape(B, S, D)
```

```python
import jax
import jax.numpy as jnp
from jax import lax
from jax.experimental import pallas as pl
from jax.experimental.pallas import tpu as pltpu

F32 = jnp.float32
BF16 = jnp.bfloat16

D_MODEL = 1024
D_POOL = 512
D_MLSTM = 512
POOL_WINDOWS = (2, 4, 8, 16)
POOL_GROUP_DIM = 128
N_HEADS = 4
HEAD_DIM = 128
CONV_WIDTH = 4
CHUNK = 128
EPS = 1e-6

LANES = 128
SUBLANES = 8
MXU_COLS = 256
POOL_HIST = CHUNK
POOL_FIX_ROWS = 16
CONV_HIST = SUBLANES
GATE_PAD = LANES

OFF_U = 0
OFF_ZP = 512
OFF_QK = 1024
OFF_V = 2048
OFF_G = 3584

R_ZP = 0
R_V = 512
R_O = 1024
R_ZM = 1536
R_COLS = 2048

G_A = 0
G_W = SUBLANES
G_DECAY = 2 * SUBLANES
G_ROWS = 3 * SUBLANES

GC_M = 0
GC_INTER = SUBLANES
GC_EFLOOR = 2 * SUBLANES

SEQ_TILE = 512
N_CHUNKS = SEQ_TILE // CHUNK
STRIP = 128
NORM_STRIP = 16
NORM_ROWS = 64
OUT_ROWS = 512
PROJ_ROWS = 256
PROJ_COLS = 2 * MXU_COLS
PIPE_DEPTH = 3
N_PAIRED = 7
N_HEAD_UNITS = N_CHUNKS * N_HEADS
IN_FLIGHT = 3
ROUNDS_PER_TILE = 44
POOL_ORDER = (0, 1, 0, 2, 1, 3, 2, 3)
VMEM_LIMIT = 56 * 1024 * 1024


def _sigmoid(z):
    return 0.5 * jnp.tanh(0.5 * z) + 0.5


def _silu(z):
    hz = 0.5 * z
    return hz * jnp.tanh(hz) + hz


def _log_sigmoid(z):
    return jnp.minimum(z, 0.0) - jnp.log1p(jnp.exp(-jnp.abs(z)))


def _strips(n, step):
    return [slice(r, r + step) for r in range(0, n, step)]


def _merge_evenly(lists):
    merged = []
    total = sum(len(l) for l in lists)
    taken = [0] * len(lists)
    for i in range(1, total + 1):
        for j, l in enumerate(lists):
            while taken[j] < (i * len(l)) // total:
                merged.append(l[taken[j]])
                taken[j] += 1
    return merged


def _ada_kernel(c_ref, w_ref, b_ref, o_ref):
    c = c_ref[...]
    o_ref[...] = jnp.dot(_silu(c), w_ref[...], preferred_element_type=F32) + b_ref[...]


def _proj_pieces():
    pieces = []
    for lo in range(OFF_U, OFF_ZP, PROJ_COLS):
        pieces.append((lo, PROJ_COLS, "u", lo - OFF_U))
    for lo in range(OFF_ZP, OFF_QK, PROJ_COLS):
        pieces.append((lo, PROJ_COLS, "rest", R_ZP + lo - OFF_ZP))
    for lo in range(OFF_QK, OFF_V, PROJ_COLS):
        pieces.append((lo, PROJ_COLS, "qk", lo - OFF_QK))
    for lo in range(OFF_V, OFF_G, PROJ_COLS):
        pieces.append((lo, PROJ_COLS, "v" if lo == OFF_V else "rest", R_V + lo - OFF_V))
    return pieces


def _make_block_kernel(tiles_per_seq, n_tiles):
    T = SEQ_TILE

    def block_kernel(x_ref, xres_ref, mod_ref, ng_ref, win_ref, wg_ref, bg_ref, cw_ref, cb_ref, wp_ref,
                     ls_ref, mg_ref, wout_ref, fg_ref, o_ref, *scratch):
        s = pl.program_id(0)
        pairs, shared = scratch[:2 * N_PAIRED], scratch[2 * N_PAIRED:]
        band_ref, caug_ref, m_ref, rhs2_ref = shared[0], shared[2], shared[3], shared[-1]

        @pl.when(s == 0)
        def _():
            for ref in pairs + (caug_ref, m_ref):
                ref[...] = jnp.zeros(ref.shape, ref.dtype)
            rhs2_ref[:, 0:CHUNK, HEAD_DIM:] = jnp.ones((N_HEAD_UNITS, CHUNK, HEAD_DIM), BF16)
            t_i = lax.broadcasted_iota(jnp.int32, (CHUNK, 2 * CHUNK), 0)
            r_i = lax.broadcasted_iota(jnp.int32, (CHUNK, 2 * CHUNK), 1) - POOL_HIST
            for g, w in enumerate(POOL_WINDOWS):
                in_window = (r_i <= t_i) & (r_i > t_i - w)
                band = jnp.where(in_window, 1.0 / w, 0.0) - jnp.where(r_i == t_i, 1.0, 0.0)
                band_ref[g] = band.astype(BF16)

        inputs = (x_ref, xres_ref, mod_ref, ng_ref, win_ref, wg_ref, bg_ref, cw_ref, cb_ref, wp_ref,
                  ls_ref, mg_ref, wout_ref, fg_ref, o_ref)
        for parity in range(2):
            written = pairs[parity::2]
            read = pairs[1 - parity::2]
            bufs = tuple(r for wr in zip(written, read) for r in wr)

            @pl.when(s % 2 == parity)
            def _(bufs=bufs):
                step(s, *inputs, *bufs, *shared)

    def step(s, x_ref, xres_ref, mod_ref, ng_ref, win_ref, wg_ref, bg_ref, cw_ref, cb_ref, wp_ref,
             ls_ref, mg_ref, wout_ref, fg_ref, o_ref,
             h_w, h_r, u_w, u_r, qk_w, qk_r, rest_w, rest_r, gcol_w, gcol_r, grow_w, grow_r,
             vbf_w, vbf_r,
             band_ref, mix_ref, caug_ref, m_ref, pooled_ref, kf_ref, ktbf_ref, kwt_ref, qbf_ref,
             lhs2_ref, rhs2_ref):
        tile_n = jnp.minimum(s, n_tiles - 1)
        tile_c = jnp.clip(s - (PIPE_DEPTH - 1), 0, n_tiles - 1)

        first_of_seq = (s - (PIPE_DEPTH - 1)) % tiles_per_seq == 0
        first_of_seq_g = (s - 1) % tiles_per_seq == 0

        @pl.when(first_of_seq)
        def _():
            u_r[0:POOL_HIST, :] = jnp.zeros((POOL_HIST, D_POOL), BF16)
            qk_r[0:CONV_HIST, :] = jnp.zeros((CONV_HIST, 2 * D_MLSTM), F32)
            caug_ref[...] = jnp.zeros(caug_ref.shape, F32)

        mod_n = mod_ref[tile_n // tiles_per_seq]
        shift = mod_n[0:1, :]
        gscale = ng_ref[...] * (1.0 + mod_n[1:2, :])
        gate = mod_ref[tile_c // tiles_per_seq][2:3, :]

        def norm_block(i):
            for rs in _strips(NORM_ROWS, NORM_STRIP):
                rows = slice(i * NORM_ROWS + rs.start, i * NORM_ROWS + rs.stop)
                x = x_ref[rows, :]
                ms = jnp.mean(x * x, axis=-1, keepdims=True)
                h_w[rows, :] = (x * lax.rsqrt(ms + EPS) * gscale + shift).astype(BF16)

        def proj_piece(piece, rb):
            lo, width, dst, col = piece
            r0, r1 = rb * PROJ_ROWS, (rb + 1) * PROJ_ROWS
            y = jnp.dot(h_r[r0:r1, :], win_ref[:, lo:lo + width],
                        preferred_element_type=F32)
            if dst == "u":
                u_w[POOL_HIST + r0:POOL_HIST + r1, col:col + width] = y.astype(BF16)
            elif dst == "v":
                vbf_w[r0:r1, :] = y.astype(BF16)
            elif dst == "qk":
                qk_w[CONV_HIST + r0:CONV_HIST + r1, col:col + width] = y
            else:
                rest_w[r0:r1, col:col + width] = y

        zero_rows = jnp.zeros((CHUNK - 3 * SUBLANES, CHUNK), F32)

        def gates_tile():
            gts = []
            for rb in range(T // PROJ_ROWS):
                g_pre = jnp.dot(h_r[rb * PROJ_ROWS:(rb + 1) * PROJ_ROWS, :],
                                wg_ref[...],
                                preferred_element_type=F32) + bg_ref[...]
                for cc in range(PROJ_ROWS // CHUNK):
                    gts.append(g_pre[cc * CHUNK:(cc + 1) * CHUNK, :].T[0:SUBLANES, :])
            yield
            lane = lax.broadcasted_iota(jnp.int32, (SUBLANES, CHUNK), 1)
            bs = [pltpu.roll(_log_sigmoid(gt), N_HEADS, axis=0) for gt in gts]
            k = 1
            while k < CHUNK:
                bs = [x + jnp.where(lane >= k, pltpu.roll(x, k, axis=1), 0.0) for x in bs]
                k *= 2
                yield
            a_s = [gt - b for gt, b in zip(gts, bs)]
            cm = a_s
            k = 1
            while k < CHUNK:
                cm = [jnp.maximum(x, jnp.where(lane >= k, pltpu.roll(x, k, axis=1), -jnp.inf))
                      for x in cm]
                k *= 2
                yield
            m_run = jnp.where(first_of_seq_g, 0.0, m_ref[...])
            for c in range(N_CHUNKS):
                a, b = a_s[c], bs[c]
                big_m = jnp.maximum(m_run, cm[c])
                m_last = jnp.broadcast_to(big_m[:, CHUNK - 1:CHUNK], big_m.shape)
                b_last = jnp.broadcast_to(b[:, CHUNK - 1:CHUNK], b.shape)
                yield
                inter = jnp.exp(m_run - big_m)
                efloor = jnp.exp(-(b + big_m))
                grow_w[c, G_A:G_A + SUBLANES, :] = a
                grow_w[c, G_W:G_W + SUBLANES, :] = jnp.exp(a - m_last)
                grow_w[c, G_DECAY:G_DECAY + SUBLANES, :] = jnp.exp(m_run - m_last)
                gcol_w[c] = jnp.concatenate([big_m, inter, efloor, zero_rows], axis=0).T
                m_run = b_last + m_last
            m_ref[...] = m_run

        def pool_chunk(c, slot):
            r0 = c * CHUNK
            for g, w in enumerate(POOL_WINDOWS):
                cs = slice(g * POOL_GROUP_DIM, (g + 1) * POOL_GROUP_DIM)
                pg = jnp.dot(band_ref[g], u_r[r0:r0 + 2 * CHUNK, cs],
                             preferred_element_type=F32)
                if c == 0:
                    cur = u_r[POOL_HIST:POOL_HIST + POOL_FIX_ROWS, cs].astype(F32)
                    pos = lax.broadcasted_iota(jnp.int32, (POOL_FIX_ROWS, POOL_GROUP_DIM), 0) + 1
                    short = float(w) / jnp.minimum(pos, w).astype(F32) - 1.0
                    scale = 1.0 + first_of_seq.astype(F32) * short
                    fixed = (pg[0:POOL_FIX_ROWS] + cur) * scale - cur
                    pooled_ref[slot, 0:POOL_FIX_ROWS, cs] = fixed.astype(BF16)
                    pooled_ref[slot, POOL_FIX_ROWS:, cs] = pg[POOL_FIX_ROWS:].astype(BF16)
                else:
                    pooled_ref[slot, :, cs] = pg.astype(BF16)
            yield
            for pair in range(len(POOL_WINDOWS) // 2):
                cs = slice(pair * MXU_COLS, (pair + 1) * MXU_COLS)
                y = jnp.dot(pooled_ref[slot, :, cs], wp_ref[pair], preferred_element_type=F32)
                for rs in _strips(CHUNK, STRIP):
                    rows = slice(r0 + rs.start, r0 + rs.stop)
                    zp = rest_r[rows, R_ZP + pair * MXU_COLS:R_ZP + (pair + 1) * MXU_COLS]
                    mix_ref[rows, cs] = (y[rs] * ls_ref[:, cs] * _silu(zp)).astype(BF16)

        def head_chunk(c, hd, slot):
            r0 = c * CHUNK
            hs = slice(hd * HEAD_DIM, (hd + 1) * HEAD_DIM)
            ks = slice(D_MLSTM + hd * HEAD_DIM, D_MLSTM + (hd + 1) * HEAD_DIM)

            def conv(cols_, rs):
                n = STRIP // SUBLANES
                base = CONV_HIST + r0 + rs.start - SUBLANES
                vs = [qk_r[base + SUBLANES * i:base + SUBLANES * (i + 1), cols_] for i in range(n + 1)]
                row = lax.broadcasted_iota(jnp.int32, (SUBLANES, HEAD_DIM), 0)
                w_now = cw_ref[CONV_WIDTH - 1:CONV_WIDTH, cols_]
                ys = [cb_ref[:, cols_] + w_now * vs[i + 1] for i in range(n)]
                for k in range(1, CONV_WIDTH):
                    w_k = cw_ref[CONV_WIDTH - 1 - k:CONV_WIDTH - k, cols_]
                    rolled = [pltpu.roll(v, k, axis=0) for v in vs]
                    ys = [ys[i] + w_k * jnp.where(row >= k, rolled[i + 1], rolled[i]) for i in range(n)]
                return _silu(jnp.concatenate(ys, axis=0))

            def gcol(col, rs):
                return gcol_r[c, rs, col + hd:col + hd + 1]

            for rs in _strips(CHUNK, STRIP):
                kf_ref[slot, rs, :] = conv(ks, rs) * (HEAD_DIM ** -0.5)
            yield
            kt = kf_ref[slot].T
            w_row = grow_r[c, G_W + hd:G_W + hd + 1, :]
            for rs in _strips(HEAD_DIM, STRIP):
                ktbf_ref[slot, rs, :] = kt[rs].astype(BF16)
                kwt_ref[slot, rs, :] = (kt[rs] * w_row).astype(BF16)
                rhs2_ref[slot, CHUNK + rs.start:CHUNK + rs.stop, :] = caug_ref[hd, rs, :].astype(BF16)
            yield
            for rs in _strips(CHUNK, STRIP):
                rows = slice(r0 + rs.start, r0 + rs.stop)
                q = conv(hs, rs)
                qbf_ref[slot, rs, :] = q.astype(BF16)
                lhs2_ref[slot, rs, CHUNK:] = (q * gcol(GC_INTER, rs)).astype(BF16)
                rhs2_ref[slot, rs, 0:HEAD_DIM] = vbf_r[rows, hs]
            yield
            sc = jnp.dot(qbf_ref[slot], ktbf_ref[slot], preferred_element_type=F32)
            upd = jnp.dot(kwt_ref[slot], rhs2_ref[slot, 0:CHUNK, :], preferred_element_type=F32)
            dec = grow_r[c, G_DECAY + hd:G_DECAY + hd + 1, :]
            dec2 = jnp.concatenate([dec, dec], axis=1)
            for rs in _strips(HEAD_DIM, STRIP):
                caug_ref[hd, rs, :] = dec2 * caug_ref[hd, rs, :] + upd[rs]
            yield
            a_row = grow_r[c, G_A + hd:G_A + hd + 1, :]
            for rs in _strips(CHUNK, STRIP):
                row_i = lax.broadcasted_iota(jnp.int32, (STRIP, CHUNK), 0) + rs.start
                col_i = lax.broadcasted_iota(jnp.int32, (STRIP, CHUNK), 1)
                dmat = jnp.where(col_i <= row_i, jnp.exp(a_row - gcol(GC_M, rs)), 0.0)
                lhs2_ref[slot, rs, 0:CHUNK] = (sc[rs] * dmat).astype(BF16)
            yield
            numden = jnp.dot(lhs2_ref[slot], rhs2_ref[slot], preferred_element_type=F32)
            yield
            for rs in _strips(CHUNK, STRIP):
                rows = slice(r0 + rs.start, r0 + rs.stop)
                num = numden[rs, :HEAD_DIM]
                den = numden[rs, HEAD_DIM:]
                hm = num / jnp.maximum(jnp.abs(den), gcol(GC_EFLOOR, rs))
                ms = jnp.mean(hm * hm, axis=-1, keepdims=True)
                og = _sigmoid(rest_r[rows, R_O + hd * HEAD_DIM:R_O + (hd + 1) * HEAD_DIM])
                zm = rest_r[rows, R_ZM + hd * HEAD_DIM:R_ZM + (hd + 1) * HEAD_DIM]
                hm = hm * lax.rsqrt(ms + EPS) * mg_ref[:, hs]
                mix_ref[rows, D_POOL + hd * HEAD_DIM:D_POOL + (hd + 1) * HEAD_DIM] = (
                    hm * og * _silu(zm)).astype(BF16)

        def out_block(rb):
            rows = slice(rb * OUT_ROWS, (rb + 1) * OUT_ROWS)
            y = jnp.dot(mix_ref[rows, :], wout_ref[...], preferred_element_type=F32)
            o_ref[rows, :] = xres_ref[rows, :] + gate * y
            for rs in _strips(OUT_ROWS, NORM_STRIP):
                rows = slice(rb * OUT_ROWS + rs.start, rb * OUT_ROWS + rs.stop)
                z = o_ref[rows, :]
                ms = jnp.mean(z * z, axis=-1, keepdims=True)
                o_ref[rows, :] = z * lax.rsqrt(ms + EPS) * fg_ref[...]

        proj_fill = [lambda p=p, rb=rb: proj_piece(p, rb)
                     for p in _proj_pieces() for rb in range(T // PROJ_ROWS)]
        pools = [pool_chunk(c, c) for c in range(N_CHUNKS)]
        pool_fill = [lambda c=c: next(pools[c], None) for c in POOL_ORDER]
        fill = _merge_evenly([proj_fill, pool_fill])
        norm_fill = [lambda i=i: norm_block(i) for i in range(T // NORM_ROWS)]

        def emit(work, n):
            for _ in range(n):
                if work:
                    work.pop(0)()

        def advance(gen):
            try:
                next(gen)
                return True
            except StopIteration:
                return False

        units = [head_chunk(c, hd, c * N_HEADS + hd)
                 for c in range(N_CHUNKS) for hd in range(N_HEADS)]
        n_rounds = ROUNDS_PER_TILE
        len_fill = len(fill)
        gates = gates_tile()
        advance(gates)
        active = []
        rounds = 0
        while units or active:
            while units and len(active) < IN_FLIGHT:
                active.append(units.pop(0))
            active = [gen for gen in active if advance(gen)]
            rounds += 1
            emit(fill, (rounds * len_fill) // n_rounds - ((rounds - 1) * len_fill) // n_rounds)
            advance(gates)
        for _ in gates:
            pass
        emit(fill, len(fill))

        u_w[0:POOL_HIST, :] = u_r[T:T + POOL_HIST, :]
        qk_w[0:CONV_HIST, :] = qk_r[T:T + CONV_HIST, :]

        for rb in range(T // OUT_ROWS):
            emit(norm_fill, len(norm_fill) // (T // OUT_ROWS))
            out_block(rb)
        emit(norm_fill, len(norm_fill))

    return block_kernel


def _const_spec(shape):
    return pl.BlockSpec(shape, lambda s: (0,) * len(shape), pipeline_mode=pl.Buffered(1))


@jax.jit
def kernel(x, c, norm_g, w_ada, b_ada, w_in, b_gates, conv_w, conv_b, w_pool, ls_pool,
           mh_norm_g, w_out, final_g):
    B, S, D = x.shape
    T = SEQ_TILE
    assert D == D_MODEL and S % T == 0
    assert w_in.shape == (1, D_MODEL, OFF_G + 2 * N_HEADS)
    tiles_per_seq = S // T
    n_tiles = B * tiles_per_seq

    n_ada = 3
    mod = pl.pallas_call(
        _ada_kernel,
        grid=(n_ada,),
        in_specs=[pl.BlockSpec((B, D), lambda n: (0, 0)),
                  pl.BlockSpec((None, D, D), lambda n: (0, 0, n)),
                  pl.BlockSpec((1, D), lambda n: (0, n))],
        out_specs=pl.BlockSpec((B, D), lambda n: (0, n)),
        out_shape=jax.ShapeDtypeStruct((B, n_ada * D), F32),
    )(c, w_ada, b_ada)
    mod = mod.reshape(B, n_ada, D)

    pad = GATE_PAD - 2 * N_HEADS
    w_main = w_in[0].astype(BF16)
    w_gates = jnp.pad(w_in[0][:, OFF_G:], ((0, 0), (0, pad))).astype(BF16)
    b_gates_p = jnp.pad(b_gates, ((0, 0), (0, pad)))
    n_pairs = len(POOL_WINDOWS) // 2
    wp = w_pool[0].astype(BF16).reshape(n_pairs, 2, POOL_GROUP_DIM, POOL_GROUP_DIM)
    zeros = jnp.zeros((n_pairs, POOL_GROUP_DIM, POOL_GROUP_DIM), BF16)
    wp_bd = jnp.concatenate([jnp.concatenate([wp[:, 0], zeros], axis=2),
                             jnp.concatenate([zeros, wp[:, 1]], axis=2)], axis=1)
    x_tiles = x.reshape(n_tiles, T, D)

    def tile_n(s):
        return (jnp.minimum(s, n_tiles - 1), 0, 0)

    def tile_c(s):
        return (jnp.clip(s - (PIPE_DEPTH - 1), 0, n_tiles - 1), 0, 0)

    out = pl.pallas_call(
        _make_block_kernel(tiles_per_seq, n_tiles),
        grid=(n_tiles + PIPE_DEPTH - 1,),
        in_specs=[
            pl.BlockSpec((None, T, D), tile_n),
            pl.BlockSpec((None, T, D), tile_c),
            _const_spec((B, n_ada, D)),
            _const_spec((1, D)),
            _const_spec((D, OFF_G + 2 * N_HEADS)),
            _const_spec((D, GATE_PAD)),
            _const_spec((1, GATE_PAD)),
            _const_spec((CONV_WIDTH, 2 * D_MLSTM)),
            _const_spec((1, 2 * D_MLSTM)),
            _const_spec((n_pairs, MXU_COLS, MXU_COLS)),
            _const_spec((1, D_POOL)),
            _const_spec((1, D_MLSTM)),
            _const_spec((D, D)),
            _const_spec((1, D)),
        ],
        out_specs=pl.BlockSpec((None, T, D), tile_c),
        out_shape=jax.ShapeDtypeStruct((n_tiles, T, D), x.dtype),
        scratch_shapes=[
            *2 * [pltpu.VMEM((T, D), BF16)],
            *2 * [pltpu.VMEM((POOL_HIST + T, D_POOL), BF16)],
            *2 * [pltpu.VMEM((CONV_HIST + T, 2 * D_MLSTM), F32)],
            *2 * [pltpu.VMEM((T, R_COLS), F32)],
            *2 * [pltpu.VMEM((N_CHUNKS, CHUNK, CHUNK), F32)],
            *2 * [pltpu.VMEM((N_CHUNKS, G_ROWS, CHUNK), F32)],
            *2 * [pltpu.VMEM((T, D_MLSTM), BF16)],
            pltpu.VMEM((len(POOL_WINDOWS), CHUNK, 2 * CHUNK), BF16),
            pltpu.VMEM((T, D), BF16),
            pltpu.VMEM((N_HEADS, HEAD_DIM, 2 * HEAD_DIM), F32),
            pltpu.VMEM((SUBLANES, CHUNK), F32),
            pltpu.VMEM((N_CHUNKS, CHUNK, D_POOL), BF16),
            pltpu.VMEM((N_HEAD_UNITS, CHUNK, HEAD_DIM), F32),
            pltpu.VMEM((N_HEAD_UNITS, HEAD_DIM, CHUNK), BF16),
            pltpu.VMEM((N_HEAD_UNITS, HEAD_DIM, CHUNK), BF16),
            pltpu.VMEM((N_HEAD_UNITS, CHUNK, HEAD_DIM), BF16),
            pltpu.VMEM((N_HEAD_UNITS, CHUNK, CHUNK + HEAD_DIM), BF16),
            pltpu.VMEM((N_HEAD_UNITS, CHUNK + HEAD_DIM, 2 * HEAD_DIM), BF16),
        ],
        compiler_params=pltpu.CompilerParams(
            dimension_semantics=("arbitrary",),
            vmem_limit_bytes=VMEM_LIMIT),
    )(x_tiles, x_tiles, mod, norm_g, w_main, w_gates, b_gates_p, conv_w[0], conv_b,
      wp_bd, ls_pool, mh_norm_g, w_out[0].astype(BF16), final_g.reshape(1, D))
    return out.reshape(B, S, D)
```

```python
import jax
import jax.numpy as jnp
from jax import lax
from jax.experimental import pallas as pl
from jax.experimental.pallas import tpu as pltpu

F32 = jnp.float32
BF16 = jnp.bfloat16

D_MODEL = 1024
D_POOL = 512
D_MLSTM = 512
POOL_WINDOWS = (2, 4, 8, 16)
POOL_GROUP_DIM = 128
N_HEADS = 4
HEAD_DIM = 128
CONV_WIDTH = 4
CHUNK = 128
EPS = 1e-6

LANES = 128
SUBLANES = 8
MXU_COLS = 256
POOL_HIST = CHUNK
POOL_FIX_ROWS = 16
CONV_HIST = SUBLANES
GATE_PAD = LANES

OFF_U = 0
OFF_ZP = 512
OFF_QK = 1024
OFF_V = 2048
OFF_G = 3584

R_ZP = 0
R_V = 512
R_O = 1024
R_ZM = 1536
R_COLS = 2048

G_A = 0
G_W = SUBLANES
G_DECAY = 2 * SUBLANES
G_ROWS = 3 * SUBLANES

GC_M = 0
GC_INTER = SUBLANES
GC_EFLOOR = 2 * SUBLANES

SEQ_TILE = 512
N_CHUNKS = SEQ_TILE // CHUNK
STRIP = 128
NORM_STRIP = 16
NORM_ROWS = 64
OUT_ROWS = 512
PROJ_ROWS = 256
PROJ_COLS = 2 * MXU_COLS
PIPE_DEPTH = 3
N_PAIRED = 7
N_HEAD_UNITS = N_CHUNKS * N_HEADS
IN_FLIGHT = 3
ROUNDS_PER_TILE = 44
POOL_ORDER = (0, 1, 0, 2, 1, 3, 2, 3)
VMEM_LIMIT = 56 * 1024 * 1024


def _sigmoid(z):
    return 0.5 * jnp.tanh(0.5 * z) + 0.5


def _silu(z):
    hz = 0.5 * z
    return hz * jnp.tanh(hz) + hz


def _log_sigmoid(z):
    return jnp.minimum(z, 0.0) - jnp.log1p(jnp.exp(-jnp.abs(z)))


def _strips(n, step):
    return [slice(r, r + step) for r in range(0, n, step)]


def _merge_evenly(lists):
    merged = []
    total = sum(len(l) for l in lists)
    taken = [0] * len(lists)
    for i in range(1, total + 1):
        for j, l in enumerate(lists):
            while taken[j] < (i * len(l)) // total:
                merged.append(l[taken[j]])
                taken[j] += 1
    return merged


def _ada_kernel(c_ref, w_ref, b_ref, o_ref):
    c = c_ref[...]
    o_ref[...] = jnp.dot(_silu(c), w_ref[...], preferred_element_type=F32) + b_ref[...]


def _proj_pieces():
    pieces = []
    for lo in range(OFF_U, OFF_ZP, PROJ_COLS):
        pieces.append((lo, PROJ_COLS, "u", lo - OFF_U))
    for lo in range(OFF_ZP, OFF_QK, PROJ_COLS):
        pieces.append((lo, PROJ_COLS, "rest", R_ZP + lo - OFF_ZP))
    for lo in range(OFF_QK, OFF_V, PROJ_COLS):
        pieces.append((lo, PROJ_COLS, "qk", lo - OFF_QK))
    for lo in range(OFF_V, OFF_G, PROJ_COLS):
        pieces.append((lo, PROJ_COLS, "v" if lo == OFF_V else "rest", R_V + lo - OFF_V))
    return pieces


def _make_block_kernel(tiles_per_seq, n_tiles):
    T = SEQ_TILE

    def block_kernel(x_ref, xres_ref, mod_ref, ng_ref, win_ref, wg_ref, bg_ref, cw_ref, cb_ref, wp_ref,
                     ls_ref, mg_ref, wout_ref, fg_ref, o_ref, *scratch):
        s = pl.program_id(0)
        pairs, shared = scratch[:2 * N_PAIRED], scratch[2 * N_PAIRED:]
        band_ref, caug_ref, m_ref, rhs2_ref = shared[0], shared[2], shared[3], shared[-1]

        @pl.when(s == 0)
        def _():
            for ref in pairs + (caug_ref, m_ref):
                ref[...] = jnp.zeros(ref.shape, ref.dtype)
            rhs2_ref[:, 0:CHUNK, HEAD_DIM:] = jnp.ones((N_HEAD_UNITS, CHUNK, HEAD_DIM), BF16)
            t_i = lax.broadcasted_iota(jnp.int32, (CHUNK, 2 * CHUNK), 0)
            r_i = lax.broadcasted_iota(jnp.int32, (CHUNK, 2 * CHUNK), 1) - POOL_HIST
            for g, w in enumerate(POOL_WINDOWS):
                in_window = (r_i <= t_i) & (r_i > t_i - w)
                band = jnp.where(in_window, 1.0 / w, 0.0) - jnp.where(r_i == t_i, 1.0, 0.0)
                band_ref[g] = band.astype(BF16)

        inputs = (x_ref, xres_ref, mod_ref, ng_ref, win_ref, wg_ref, bg_ref, cw_ref, cb_ref, wp_ref,
                  ls_ref, mg_ref, wout_ref, fg_ref, o_ref)
        for parity in range(2):
            written = pairs[parity::2]
            read = pairs[1 - parity::2]
            bufs = tuple(r for wr in zip(written, read) for r in wr)

            @pl.when(s % 2 == parity)
            def _(bufs=bufs):
                step(s, *inputs, *bufs, *shared)

    def step(s, x_ref, xres_ref, mod_ref, ng_ref, win_ref, wg_ref, bg_ref, cw_ref, cb_ref, wp_ref,
             ls_ref, mg_ref, wout_ref, fg_ref, o_ref,
             h_w, h_r, u_w, u_r, qk_w, qk_r, rest_w, rest_r, gcol_w, gcol_r, grow_w, grow_r,
             vbf_w, vbf_r,
             band_ref, mix_ref, caug_ref, m_ref, pooled_ref, kf_ref, ktbf_ref, kwt_ref, qbf_ref,
             lhs2_ref, rhs2_ref):
        tile_n = jnp.minimum(s, n_tiles - 1)
        tile_c = jnp.clip(s - (PIPE_DEPTH - 1), 0, n_tiles - 1)

        first_of_seq = (s - (PIPE_DEPTH - 1)) % tiles_per_seq == 0
        first_of_seq_g = (s - 1) % tiles_per_seq == 0

        @pl.when(first_of_seq)
        def _():
            u_r[0:POOL_HIST, :] = jnp.zeros((POOL_HIST, D_POOL), BF16)
            qk_r[0:CONV_HIST, :] = jnp.zeros((CONV_HIST, 2 * D_MLSTM), F32)
            caug_ref[...] = jnp.zeros(caug_ref.shape, F32)

        mod_n = mod_ref[tile_n // tiles_per_seq]
        shift = mod_n[0:1, :]
        gscale = ng_ref[...] * (1.0 + mod_n[1:2, :])
        gate = mod_ref[tile_c // tiles_per_seq][2:3, :]

        def norm_block(i):
            for rs in _strips(NORM_ROWS, NORM_STRIP):
                rows = slice(i * NORM_ROWS + rs.start, i * NORM_ROWS + rs.stop)
                x = x_ref[rows, :]
                ms = jnp.mean(x * x, axis=-1, keepdims=True)
                h_w[rows, :] = (x * lax.rsqrt(ms + EPS) * gscale + shift).astype(BF16)

        def proj_piece(piece, rb):
            lo, width, dst, col = piece
            r0, r1 = rb * PROJ_ROWS, (rb + 1) * PROJ_ROWS
            y = jnp.dot(h_r[r0:r1, :], win_ref[:, lo:lo + width],
                        preferred_element_type=F32)
            if dst == "u":
                u_w[POOL_HIST + r0:POOL_HIST + r1, col:col + width] = y.astype(BF16)
            elif dst == "v":
                vbf_w[r0:r1, :] = y.astype(BF16)
            elif dst == "qk":
                qk_w[CONV_HIST + r0:CONV_HIST + r1, col:col + width] = y
            else:
                rest_w[r0:r1, col:col + width] = y

        zero_rows = jnp.zeros((CHUNK - 3 * SUBLANES, CHUNK), F32)

        def gates_tile():
            gts = []
            for rb in range(T // PROJ_ROWS):
                g_pre = jnp.dot(h_r[rb * PROJ_ROWS:(rb + 1) * PROJ_ROWS, :],
                                wg_ref[...],
                                preferred_element_type=F32) + bg_ref[...]
                for cc in range(PROJ_ROWS // CHUNK):
                    gts.append(g_pre[cc * CHUNK:(cc + 1) * CHUNK, :].T[0:SUBLANES, :])
            yield
            lane = lax.broadcasted_iota(jnp.int32, (SUBLANES, CHUNK), 1)
            bs = [pltpu.roll(_log_sigmoid(gt), N_HEADS, axis=0) for gt in gts]
            k = 1
            while k < CHUNK:
                bs = [x + jnp.where(lane >= k, pltpu.roll(x, k, axis=1), 0.0) for x in bs]
                k *= 2
                yield
            a_s = [gt - b for gt, b in zip(gts, bs)]
            cm = a_s
            k = 1
            while k < CHUNK:
                cm = [jnp.maximum(x, jnp.where(lane >= k, pltpu.roll(x, k, axis=1), -jnp.inf))
                      for x in cm]
                k *= 2
                yield
            m_run = jnp.where(first_of_seq_g, 0.0, m_ref[...])
            for c in range(N_CHUNKS):
                a, b = a_s[c], bs[c]
                big_m = jnp.maximum(m_run, cm[c])
                m_last = jnp.broadcast_to(big_m[:, CHUNK - 1:CHUNK], big_m.shape)
                b_last = jnp.broadcast_to(b[:, CHUNK - 1:CHUNK], b.shape)
                yield
                inter = jnp.exp(m_run - big_m)
                efloor = jnp.exp(-(b + big_m))
                grow_w[c, G_A:G_A + SUBLANES, :] = a
                grow_w[c, G_W:G_W + SUBLANES, :] = jnp.exp(a - m_last)
                grow_w[c, G_DECAY:G_DECAY + SUBLANES, :] = jnp.exp(m_run - m_last)
                gcol_w[c] = jnp.concatenate([big_m, inter, efloor, zero_rows], axis=0).T
                m_run = b_last + m_last
            m_ref[...] = m_run

        def pool_chunk(c, slot):
            r0 = c * CHUNK
            for g, w in enumerate(POOL_WINDOWS):
                cs = slice(g * POOL_GROUP_DIM, (g + 1) * POOL_GROUP_DIM)
                pg = jnp.dot(band_ref[g], u_r[r0:r0 + 2 * CHUNK, cs],
                             preferred_element_type=F32)
                if c == 0:
                    cur = u_r[POOL_HIST:POOL_HIST + POOL_FIX_ROWS, cs].astype(F32)
                    pos = lax.broadcasted_iota(jnp.int32, (POOL_FIX_ROWS, POOL_GROUP_DIM), 0) + 1
                    short = float(w) / jnp.minimum(pos, w).astype(F32) - 1.0
                    scale = 1.0 + first_of_seq.astype(F32) * short
                    fixed = (pg[0:POOL_FIX_ROWS] + cur) * scale - cur
                    pooled_ref[slot, 0:POOL_FIX_ROWS, cs] = fixed.astype(BF16)
                    pooled_ref[slot, POOL_FIX_ROWS:, cs] = pg[POOL_FIX_ROWS:].astype(BF16)
                else:
                    pooled_ref[slot, :, cs] = pg.astype(BF16)
            yield
            for pair in range(len(POOL_WINDOWS) // 2):
                cs = slice(pair * MXU_COLS, (pair + 1) * MXU_COLS)
                y = jnp.dot(pooled_ref[slot, :, cs], wp_ref[pair], preferred_element_type=F32)
                for rs in _strips(CHUNK, STRIP):
                    rows = slice(r0 + rs.start, r0 + rs.stop)
                    zp = rest_r[rows, R_ZP + pair * MXU_COLS:R_ZP + (pair + 1) * MXU_COLS]
                    mix_ref[rows, cs] = (y[rs] * ls_ref[:, cs] * _silu(zp)).astype(BF16)

        def head_chunk(c, hd, slot):
            r0 = c * CHUNK
            hs = slice(hd * HEAD_DIM, (hd + 1) * HEAD_DIM)
            ks = slice(D_MLSTM + hd * HEAD_DIM, D_MLSTM + (hd + 1) * HEAD_DIM)

            def conv(cols_, rs):
                n = STRIP // SUBLANES
                base = CONV_HIST + r0 + rs.start - SUBLANES
                vs = [qk_r[base + SUBLANES * i:base + SUBLANES * (i + 1), cols_] for i in range(n + 1)]
                row = lax.broadcasted_iota(jnp.int32, (SUBLANES, HEAD_DIM), 0)
                w_now = cw_ref[CONV_WIDTH - 1:CONV_WIDTH, cols_]
                ys = [cb_ref[:, cols_] + w_now * vs[i + 1] for i in range(n)]
                for k in range(1, CONV_WIDTH):
                    w_k = cw_ref[CONV_WIDTH - 1 - k:CONV_WIDTH - k, cols_]
                    rolled = [pltpu.roll(v, k, axis=0) for v in vs]
                    ys = [ys[i] + w_k * jnp.where(row >= k, rolled[i + 1], rolled[i]) for i in range(n)]
                return _silu(jnp.concatenate(ys, axis=0))

            def gcol(col, rs):
                return gcol_r[c, rs, col + hd:col + hd + 1]

            for rs in _strips(CHUNK, STRIP):
                kf_ref[slot, rs, :] = conv(ks, rs) * (HEAD_DIM ** -0.5)
            yield
            kt = kf_ref[slot].T
            w_row = grow_r[c, G_W + hd:G_W + hd + 1, :]
            for rs in _strips(HEAD_DIM, STRIP):
                ktbf_ref[slot, rs, :] = kt[rs].astype(BF16)
                kwt_ref[slot, rs, :] = (kt[rs] * w_row).astype(BF16)
                if c == 0:
                    rhs2_ref[slot, CHUNK + rs.start:CHUNK + rs.stop, :] = caug_ref[hd, rs, :].astype(BF16)
            yield
            for rs in _strips(CHUNK, STRIP):
                rows = slice(r0 + rs.start, r0 + rs.stop)
                q = conv(hs, rs)
                qbf_ref[slot, rs, :] = q.astype(BF16)
                lhs2_ref[slot, rs, CHUNK:] = (q * gcol(GC_INTER, rs)).astype(BF16)
                rhs2_ref[slot, rs, 0:HEAD_DIM] = vbf_r[rows, hs]
            yield
            sc = jnp.dot(qbf_ref[slot], ktbf_ref[slot], preferred_element_type=F32)
            upd = jnp.dot(kwt_ref[slot], rhs2_ref[slot, 0:CHUNK, :], preferred_element_type=F32)
            dec = grow_r[c, G_DECAY + hd:G_DECAY + hd + 1, :]
            dec2 = jnp.concatenate([dec, dec], axis=1)
            for rs in _strips(HEAD_DIM, STRIP):
                new_state = dec2 * caug_ref[hd, rs, :] + upd[rs]
                caug_ref[hd, rs, :] = new_state
                if c + 1 < N_CHUNKS:
                    rhs2_ref[slot + N_HEADS, CHUNK + rs.start:CHUNK + rs.stop, :] = new_state.astype(BF16)
            yield
            a_row = grow_r[c, G_A + hd:G_A + hd + 1, :]
            for rs in _strips(CHUNK, STRIP):
                row_i = lax.broadcasted_iota(jnp.int32, (STRIP, CHUNK), 0) + rs.start
                col_i = lax.broadcasted_iota(jnp.int32, (STRIP, CHUNK), 1)
                dmat = jnp.where(col_i <= row_i, jnp.exp(a_row - gcol(GC_M, rs)), 0.0)
                lhs2_ref[slot, rs, 0:CHUNK] = (sc[rs] * dmat).astype(BF16)
            yield
            numden = jnp.dot(lhs2_ref[slot], rhs2_ref[slot], preferred_element_type=F32)
            yield
            for rs in _strips(CHUNK, STRIP):
                rows = slice(r0 + rs.start, r0 + rs.stop)
                num = numden[rs, :HEAD_DIM]
                den = numden[rs, HEAD_DIM:]
                hm = num / jnp.maximum(jnp.abs(den), gcol(GC_EFLOOR, rs))
                ms = jnp.mean(hm * hm, axis=-1, keepdims=True)
                og = _sigmoid(rest_r[rows, R_O + hd * HEAD_DIM:R_O + (hd + 1) * HEAD_DIM])
                zm = rest_r[rows, R_ZM + hd * HEAD_DIM:R_ZM + (hd + 1) * HEAD_DIM]
                hm = hm * lax.rsqrt(ms + EPS) * mg_ref[:, hs]
                mix_ref[rows, D_POOL + hd * HEAD_DIM:D_POOL + (hd + 1) * HEAD_DIM] = (
                    hm * og * _silu(zm)).astype(BF16)

        def out_block(rb):
            rows = slice(rb * OUT_ROWS, (rb + 1) * OUT_ROWS)
            y = jnp.dot(mix_ref[rows, :], wout_ref[...], preferred_element_type=F32)
            o_ref[rows, :] = xres_ref[rows, :] + gate * y
            for rs in _strips(OUT_ROWS, NORM_STRIP):
                rows = slice(rb * OUT_ROWS + rs.start, rb * OUT_ROWS + rs.stop)
                z = o_ref[rows, :]
                ms = jnp.mean(z * z, axis=-1, keepdims=True)
                o_ref[rows, :] = z * lax.rsqrt(ms + EPS) * fg_ref[...]

        proj_fill = [lambda p=p, rb=rb: proj_piece(p, rb)
                     for p in _proj_pieces() for rb in range(T // PROJ_ROWS)]
        pools = [pool_chunk(c, c) for c in range(N_CHUNKS)]
        pool_fill = [lambda c=c: next(pools[c], None) for c in POOL_ORDER]
        fill = _merge_evenly([proj_fill, pool_fill])
        norm_fill = [lambda i=i: norm_block(i) for i in range(T // NORM_ROWS)]

        def emit(work, n):
            for _ in range(n):
                if work:
                    work.pop(0)()

        def advance(gen):
            try:
                next(gen)
                return True
            except StopIteration:
                return False

        units = [head_chunk(c, hd, c * N_HEADS + hd)
                 for c in range(N_CHUNKS) for hd in range(N_HEADS)]
        n_rounds = ROUNDS_PER_TILE
        len_fill = len(fill)
        gates = gates_tile()
        advance(gates)
        active = []
        rounds = 0
        while units or active:
            while units and len(active) < IN_FLIGHT:
                active.append(units.pop(0))
            active = [gen for gen in active if advance(gen)]
            rounds += 1
            emit(fill, (rounds * len_fill) // n_rounds - ((rounds - 1) * len_fill) // n_rounds)
            advance(gates)
        for _ in gates:
            pass
        emit(fill, len(fill))

        u_w[0:POOL_HIST, :] = u_r[T:T + POOL_HIST, :]
        qk_w[0:CONV_HIST, :] = qk_r[T:T + CONV_HIST, :]

        for rb in range(T // OUT_ROWS):
            emit(norm_fill, len(norm_fill) // (T // OUT_ROWS))
            out_block(rb)
        emit(norm_fill, len(norm_fill))

    return block_kernel


def _const_spec(shape):
    return pl.BlockSpec(shape, lambda s: (0,) * len(shape), pipeline_mode=pl.Buffered(1))


@jax.jit
def kernel(x, c, norm_g, w_ada, b_ada, w_in, b_gates, conv_w, conv_b, w_pool, ls_pool,
           mh_norm_g, w_out, final_g):
    B, S, D = x.shape
    T = SEQ_TILE
    assert D == D_MODEL and S % T == 0
    assert w_in.shape == (1, D_MODEL, OFF_G + 2 * N_HEADS)
    tiles_per_seq = S // T
    n_tiles = B * tiles_per_seq

    n_ada = 3
    mod = pl.pallas_call(
        _ada_kernel,
        grid=(n_ada,),
        in_specs=[pl.BlockSpec((B, D), lambda n: (0, 0)),
                  pl.BlockSpec((None, D, D), lambda n: (0, 0, n)),
                  pl.BlockSpec((1, D), lambda n: (0, n))],
        out_specs=pl.BlockSpec((B, D), lambda n: (0, n)),
        out_shape=jax.ShapeDtypeStruct((B, n_ada * D), F32),
    )(c, w_ada, b_ada)
    mod = mod.reshape(B, n_ada, D)

    pad = GATE_PAD - 2 * N_HEADS
    w_main = w_in[0].astype(BF16)
    w_gates = jnp.pad(w_in[0][:, OFF_G:], ((0, 0), (0, pad))).astype(BF16)
    b_gates_p = jnp.pad(b_gates, ((0, 0), (0, pad)))
    n_pairs = len(POOL_WINDOWS) // 2
    wp = w_pool[0].astype(BF16).reshape(n_pairs, 2, POOL_GROUP_DIM, POOL_GROUP_DIM)
    zeros = jnp.zeros((n_pairs, POOL_GROUP_DIM, POOL_GROUP_DIM), BF16)
    wp_bd = jnp.concatenate([jnp.concatenate([wp[:, 0], zeros], axis=2),
                             jnp.concatenate([zeros, wp[:, 1]], axis=2)], axis=1)
    x_tiles = x.reshape(n_tiles, T, D)

    def tile_n(s):
        return (jnp.minimum(s, n_tiles - 1), 0, 0)

    def tile_c(s):
        return (jnp.clip(s - (PIPE_DEPTH - 1), 0, n_tiles - 1), 0, 0)

    out = pl.pallas_call(
        _make_block_kernel(tiles_per_seq, n_tiles),
        grid=(n_tiles + PIPE_DEPTH - 1,),
        in_specs=[
            pl.BlockSpec((None, T, D), tile_n),
            pl.BlockSpec((None, T, D), tile_c),
            _const_spec((B, n_ada, D)),
            _const_spec((1, D)),
            _const_spec((D, OFF_G + 2 * N_HEADS)),
            _const_spec((D, GATE_PAD)),
            _const_spec((1, GATE_PAD)),
            _const_spec((CONV_WIDTH, 2 * D_MLSTM)),
            _const_spec((1, 2 * D_MLSTM)),
            _const_spec((n_pairs, MXU_COLS, MXU_COLS)),
            _const_spec((1, D_POOL)),
            _const_spec((1, D_MLSTM)),
            _const_spec((D, D)),
            _const_spec((1, D)),
        ],
        out_specs=pl.BlockSpec((None, T, D), tile_c),
        out_shape=jax.ShapeDtypeStruct((n_tiles, T, D), x.dtype),
        scratch_shapes=[
            *2 * [pltpu.VMEM((T, D), BF16)],
            *2 * [pltpu.VMEM((POOL_HIST + T, D_POOL), BF16)],
            *2 * [pltpu.VMEM((CONV_HIST + T, 2 * D_MLSTM), F32)],
            *2 * [pltpu.VMEM((T, R_COLS), F32)],
            *2 * [pltpu.VMEM((N_CHUNKS, CHUNK, CHUNK), F32)],
            *2 * [pltpu.VMEM((N_CHUNKS, G_ROWS, CHUNK), F32)],
            *2 * [pltpu.VMEM((T, D_MLSTM), BF16)],
            pltpu.VMEM((len(POOL_WINDOWS), CHUNK, 2 * CHUNK), BF16),
            pltpu.VMEM((T, D), BF16),
            pltpu.VMEM((N_HEADS, HEAD_DIM, 2 * HEAD_DIM), F32),
            pltpu.VMEM((SUBLANES, CHUNK), F32),
            pltpu.VMEM((N_CHUNKS, CHUNK, D_POOL), BF16),
            pltpu.VMEM((N_HEAD_UNITS, CHUNK, HEAD_DIM), F32),
            pltpu.VMEM((N_HEAD_UNITS, HEAD_DIM, CHUNK), BF16),
            pltpu.VMEM((N_HEAD_UNITS, HEAD_DIM, CHUNK), BF16),
            pltpu.VMEM((N_HEAD_UNITS, CHUNK, HEAD_DIM), BF16),
            pltpu.VMEM((N_HEAD_UNITS, CHUNK, CHUNK + HEAD_DIM), BF16),
            pltpu.VMEM((N_HEAD_UNITS, CHUNK + HEAD_DIM, 2 * HEAD_DIM), BF16),
        ],
        compiler_params=pltpu.CompilerParams(
            dimension_semantics=("arbitrary",),
            vmem_limit_bytes=VMEM_LIMIT),
    )(x_tiles, x_tiles, mod, norm_g, w_main, w_gates, b_gates_p, conv_w[0], conv_b,
      wp_bd, ls_pool, mh_norm_g, w_out[0].astype(BF16), final_g.reshape(1, D))
    return out.reshape(B, S, D)
```
